```python
import jax, jax.numpy as jnp
from jax import lax
import numpy as np

D_MODEL = 1024
BATCH = 32
SEQ = 2048
DEPTH = 4

CTX_LEN = 256
GRID_W = 64
HEAD_DIM = 64
BRANCH_WIDTH = 512
N_BRANCH = 3
CONV_WIDTH = BRANCH_WIDTH
CONV_K = 3
NA_HEADS = BRANCH_WIDTH // HEAD_DIM
NA_WIDTH = NA_HEADS * HEAD_DIM
NA_ROWS_MAX = 8
NA_COLS = 16
NA_QCOL_BLOCK = 16
NA_KCOL_BLOCK = 32
SW_Q_HEADS = BRANCH_WIDTH // HEAD_DIM
SW_KV_HEADS = 2
SW_Q_WIDTH = SW_Q_HEADS * HEAD_DIM
SW_KV_WIDTH = SW_KV_HEADS * HEAD_DIM
SW_WINDOW = 128
SW_BLOCK = 128
ROPE_BASE = 10000.0
ROPE_AXIS_FREQS = HEAD_DIM // 4
N_EXPERTS = 16
EXPERT_FF = 1024
CAPACITY_FACTOR = 2
RMS_EPS = 1e-6
NEG = -1e30
IN_SPLITS = [CONV_WIDTH] * 3 + [NA_WIDTH] * 3 + [SW_Q_WIDTH, SW_KV_WIDTH, SW_KV_WIDTH] + [D_MODEL] * N_BRANCH
IN_WIDTH = 3 * CONV_WIDTH + 3 * NA_WIDTH + SW_Q_WIDTH + 2 * SW_KV_WIDTH + N_BRANCH * D_MODEL

kernel_name = 'hybrid_dit_conv_natten_swa_ec_moe'


def rms_norm(x, g):
    xf = x.astype(jnp.float32)
    y = xf * lax.rsqrt(jnp.mean(xf * xf, axis=-1, keepdims=True) + RMS_EPS)
    return (y * g.astype(jnp.float32)).astype(x.dtype)


def modulate(h, shift, scale):
    return h * (1 + scale) + shift


def split_in(z):
    offs, acc = [], 0
    for s in IN_SPLITS[:-1]:
        acc += s
        offs.append(acc)
    return jnp.split(z, offs, axis=-1)


def heads(t, n):
    return t.reshape(t.shape[0], t.shape[1], n, HEAD_DIM)


def axial_rope(n_tokens):
    t = jnp.arange(n_tokens, dtype=jnp.int32)
    row = (t // GRID_W).astype(jnp.float32)
    col = (t % GRID_W).astype(jnp.float32)
    inv = ROPE_BASE ** (-jnp.arange(ROPE_AXIS_FREQS, dtype=jnp.float32) / ROPE_AXIS_FREQS)
    ang = jnp.concatenate([row[:, None] * inv, col[:, None] * inv], axis=-1)
    return jnp.cos(ang), jnp.sin(ang)


def apply_rope(x, cos, sin):
    xf = x.astype(jnp.float32)
    half = HEAD_DIM // 2
    x1, x2 = xf[..., :half], xf[..., half:]
    cs, sn = cos[None, :, None, :], sin[None, :, None, :]
    return jnp.concatenate([x1 * cs - x2 * sn, x1 * sn + x2 * cs], axis=-1).astype(x.dtype)


def short_conv(u, w):
    return lax.conv_general_dilated(u, w[:, None, :], window_strides=(1,),
                                    padding=((CONV_K // 2, CONV_K // 2),),
                                    dimension_numbers=('NWC', 'WIO', 'NWC'),
                                    feature_group_count=u.shape[-1])


def context_attention(q, k, v, sink):
    B, L, H, Dh = q.shape
    Hk = k.shape[2]
    G = H // Hk
    qg = q.reshape(B, L, Hk, G, Dh)
    s = jnp.einsum('bqkgd,blkd->bkgql', qg, k).astype(jnp.float32) * (Dh ** -0.5)
    if sink is not None:
        sk = jnp.broadcast_to(sink.astype(jnp.float32).reshape(Hk, G, 1, 1), (B, Hk, G, L, 1))
        s = jnp.concatenate([s, sk], axis=-1)
    p = jax.nn.softmax(s, axis=-1)[..., :L].astype(v.dtype)
    o = jnp.einsum('bkgql,blkd->bqkgd', p, v)
    return o.reshape(B, L, H * Dh)


def neighbourhood_attention(q, k, v, kc, vc, rpb):
    B, S, H, Dh = q.shape
    rows = S // GRID_W
    kr = min(NA_ROWS_MAX, rows)
    ncb = GRID_W // NA_QCOL_BLOCK
    scale = Dh ** -0.5
    qcols = np.arange(GRID_W).reshape(ncb, NA_QCOL_BLOCK)
    kstart = np.clip(np.arange(ncb) * NA_QCOL_BLOCK - NA_COLS // 2, 0, GRID_W - NA_KCOL_BLOCK)
    kcols = kstart[:, None] + np.arange(NA_KCOL_BLOCK)
    cstart = np.clip(qcols - NA_COLS // 2, 0, GRID_W - NA_COLS)
    col_ok = (kcols[:, None, :] >= cstart[..., None]) & (kcols[:, None, :] < cstart[..., None] + NA_COLS)
    dcol_idx = np.clip(kcols[:, None, :] - qcols[..., None] + NA_COLS - 1, 0, 2 * NA_COLS - 2)
    qg = q.reshape(B, rows, ncb, NA_QCOL_BLOCK, H, Dh)
    kg = k.reshape(B, rows, GRID_W, H, Dh)
    vg = v.reshape(B, rows, GRID_W, H, Dh)
    n_loc = kr * NA_KCOL_BLOCK

    def row_block(r):
        rs = jnp.clip(r - kr // 2, 0, rows - kr)
        kb = lax.dynamic_slice_in_dim(kg, rs, kr, axis=1)[:, :, kcols]
        vb = lax.dynamic_slice_in_dim(vg, rs, kr, axis=1)[:, :, kcols]
        qb = lax.dynamic_index_in_dim(qg, r, axis=1, keepdims=False)
        s_loc = jnp.einsum('bjqhd,brjkhd->bhjqrk', qb, kb).astype(jnp.float32) * scale
        dr_idx = rs + jnp.arange(kr) - r + NA_ROWS_MAX - 1
        bias = rpb[:, dr_idx[:, None, None, None], dcol_idx[None]]
        bias = bias.transpose(0, 2, 3, 1, 4).astype(jnp.float32)
        s_loc = jnp.where(col_ok[:, :, None, :], s_loc + bias[None], NEG)
        s_loc = s_loc.reshape(B, H, ncb, NA_QCOL_BLOCK, n_loc)
        s_ctx = jnp.einsum('bjqhd,blhd->bhjql', qb, kc).astype(jnp.float32) * scale
        p = jax.nn.softmax(jnp.concatenate([s_loc, s_ctx], axis=-1), axis=-1).astype(v.dtype)
        p_loc = p[..., :n_loc].reshape(B, H, ncb, NA_QCOL_BLOCK, kr, NA_KCOL_BLOCK)
        return (jnp.einsum('bhjqrk,brjkhd->bjqhd', p_loc, vb)
                + jnp.einsum('bhjql,blhd->bjqhd', p[..., n_loc:], vc))

    out = lax.map(row_block, jnp.arange(rows))
    return jnp.moveaxis(out, 0, 1).reshape(B, S, H * Dh)


def window_attention(q, k, v, kc, vc, sink):
    B, S, H, Dh = q.shape
    Hk = k.shape[2]
    G = H // Hk
    L = kc.shape[1]
    nb = S // SW_BLOCK
    span = SW_BLOCK + 2 * SW_WINDOW
    scale = Dh ** -0.5
    pad = ((0, 0), (SW_WINDOW, SW_WINDOW), (0, 0), (0, 0))
    kpad, vpad = jnp.pad(k, pad), jnp.pad(v, pad)
    qall = q.reshape(B, nb, SW_BLOCK, Hk, G, Dh)
    sk = jnp.broadcast_to(sink.astype(jnp.float32).reshape(Hk, G, 1, 1), (B, Hk, G, SW_BLOCK, 1))

    def block(n):
        start = n * SW_BLOCK
        kb = lax.dynamic_slice_in_dim(kpad, start, span, axis=1)
        vb = lax.dynamic_slice_in_dim(vpad, start, span, axis=1)
        qb = lax.dynamic_index_in_dim(qall, n, axis=1, keepdims=False)
        qpos = start + jnp.arange(SW_BLOCK)
        kpos = start - SW_WINDOW + jnp.arange(span)
        ok = (jnp.abs(qpos[:, None] - kpos[None, :]) <= SW_WINDOW) & (kpos[None, :] >= 0) & (kpos[None, :] < S)
        s_loc = jnp.einsum('bqkgd,blkd->bkgql', qb, kb).astype(jnp.float32) * scale
        s_loc = jnp.where(ok, s_loc, NEG)
        s_ctx = jnp.einsum('bqkgd,blkd->bkgql', qb, kc).astype(jnp.float32) * scale
        p = jax.nn.softmax(jnp.concatenate([s_loc, s_ctx, sk], axis=-1), axis=-1).astype(v.dtype)
        return (jnp.einsum('bkgql,blkd->bqkgd', p[..., :span], vb)
                + jnp.einsum('bkgql,blkd->bqkgd', p[..., span:span + L], vc))

    out = lax.map(block, jnp.arange(nb))
    return jnp.moveaxis(out, 0, 1).reshape(B, S, H * Dh)


def merge_branches(outs, gate_logits, w_branch, w_out):
    merged = jax.nn.sigmoid(gate_logits[0]) * (outs[0] @ w_branch[0])
    for i in range(1, N_BRANCH):
        merged = merged + jax.nn.sigmoid(gate_logits[i]) * (outs[i] @ w_branch[i])
    return merged @ w_out


def expert_choice_ffn(h, w_router, w_gate, w_up, w_down):
    B, N, D = h.shape
    cap = CAPACITY_FACTOR * N // N_EXPERTS
    aff = jax.nn.softmax((h @ w_router).astype(jnp.float32), axis=-1)
    gate, idx = lax.top_k(jnp.swapaxes(aff, 1, 2), cap)
    bidx = jnp.arange(B)[:, None, None]
    xe = h[bidx, idx]
    a = jnp.einsum('becd,edf->becf', xe, w_gate)
    u = jnp.einsum('becd,edf->becf', xe, w_up)
    ye = jnp.einsum('becf,efd->becd', jax.nn.silu(a) * u, w_down) * gate[..., None].astype(h.dtype)
    return jnp.zeros_like(h).at[bidx, idx].add(ye)


def setup_inputs(seed: int = 0) -> dict:
    key = jax.random.key(seed)
    ks = jax.random.split(key, 19)
    D = D_MODEL

    def nrm(k, shape, scale):
        return jax.random.normal(k, shape, jnp.float32) * scale

    return {
        'x': nrm(ks[0], (BATCH, SEQ, D), 1.0),
        'c': nrm(ks[1], (BATCH, D), 1.0),
        'ctx': nrm(ks[2], (BATCH, CTX_LEN, D), 1.0),
        'c_ctx': nrm(ks[3], (D,), 1.0),
        'w_ada': nrm(ks[4], (DEPTH, D, 6 * D), 0.5 * D ** -0.5),
        'b_ada': nrm(ks[5], (DEPTH, 6 * D), 0.02),
        'g_mix': 1.0 + nrm(ks[6], (DEPTH, D), 0.02),
        'w_in': nrm(ks[7], (DEPTH, D, IN_WIDTH), D ** -0.5),
        'conv_w': nrm(ks[8], (DEPTH, CONV_K, CONV_WIDTH), CONV_K ** -0.5),
        'na_rpb': nrm(ks[9], (DEPTH, NA_HEADS, 2 * NA_ROWS_MAX - 1, 2 * NA_COLS - 1), 0.1),
        'sw_sink': nrm(ks[10], (DEPTH, SW_Q_HEADS), 0.5),
        'w_branch': nrm(ks[11], (DEPTH, N_BRANCH, BRANCH_WIDTH, D), BRANCH_WIDTH ** -0.5),
        'w_out': nrm(ks[12], (DEPTH, D, D), D ** -0.5),
        'g_ffn': 1.0 + nrm(ks[13], (DEPTH, D), 0.02),
        'w_router': nrm(ks[14], (DEPTH, D, N_EXPERTS), D ** -0.5),
        'w_exp_gate': nrm(ks[15], (DEPTH, N_EXPERTS, D, EXPERT_FF), D ** -0.5),
        'w_exp_up': nrm(ks[16], (DEPTH, N_EXPERTS, D, EXPERT_FF), D ** -0.5),
        'w_exp_down': nrm(ks[17], (DEPTH, N_EXPERTS, EXPERT_FF, D), EXPERT_FF ** -0.5),
        'g_final': 1.0 + nrm(ks[18], (D,), 0.02),
    }


def reference(x, c, ctx, c_ctx, w_ada, b_ada, g_mix, w_in, conv_w, na_rpb, sw_sink, w_branch, w_out,
              g_ffn, w_router, w_exp_gate, w_exp_up, w_exp_down, g_final):
    S = x.shape[1]
    cos, sin = axial_rope(S)
    xc = ctx
    for l in range(DEPTH):
        last = l == DEPTH - 1
        mod = (jax.nn.silu(c) @ w_ada[l] + b_ada[l])[:, None, :]
        mod_c = (jax.nn.silu(c_ctx)[None] @ w_ada[l] + b_ada[l])[:, None, :]
        sh1, sc1, gt1, sh2, sc2, gt2 = jnp.split(mod, 6, axis=-1)
        csh1, csc1, cgt1, csh2, csc2, cgt2 = jnp.split(mod_c, 6, axis=-1)

        h = modulate(rms_norm(x, g_mix[l]), sh1, sc1)
        hc = modulate(rms_norm(xc, g_mix[l]), csh1, csc1)
        a_b, a_c, a_x, na_q, na_k, na_v, sw_q, sw_k, sw_v, ga, gb, gc = split_in(h @ w_in[l])
        ca_b, ca_c, ca_x, cna_q, cna_k, cna_v, csw_q, csw_k, csw_v, cga, cgb, cgc = split_in(hc @ w_in[l])
        kc_na, vc_na = heads(cna_k, NA_HEADS), heads(cna_v, NA_HEADS)
        kc_sw, vc_sw = heads(csw_k, SW_KV_HEADS), heads(csw_v, SW_KV_HEADS)

        o_a = a_b * short_conv(a_c * a_x, conv_w[l])
        o_b = neighbourhood_attention(heads(na_q, NA_HEADS), heads(na_k, NA_HEADS), heads(na_v, NA_HEADS),
                                      kc_na, vc_na, na_rpb[l])
        o_c = window_attention(apply_rope(heads(sw_q, SW_Q_HEADS), cos, sin),
                               apply_rope(heads(sw_k, SW_KV_HEADS), cos, sin),
                               heads(sw_v, SW_KV_HEADS), kc_sw, vc_sw, sw_sink[l])
        x = x + gt1 * merge_branches((o_a, o_b, o_c), (ga, gb, gc), w_branch[l], w_out[l])
        if not last:
            co_a = ca_b * short_conv(ca_c * ca_x, conv_w[l])
            co_b = context_attention(heads(cna_q, NA_HEADS), kc_na, vc_na, None)
            co_c = context_attention(heads(csw_q, SW_Q_HEADS), kc_sw, vc_sw, sw_sink[l])
            xc = xc + cgt1 * merge_branches((co_a, co_b, co_c), (cga, cgb, cgc), w_branch[l], w_out[l])

        x = x + gt2 * expert_choice_ffn(modulate(rms_norm(x, g_ffn[l]), sh2, sc2),
                                        w_router[l], w_exp_gate[l], w_exp_up[l], w_exp_down[l])
        if not last:
            xc = xc + cgt2 * expert_choice_ffn(modulate(rms_norm(xc, g_ffn[l]), csh2, csc2),
                                               w_router[l], w_exp_gate[l], w_exp_up[l], w_exp_down[l])
    return rms_norm(x, g_final)
```

```python
import functools

import numpy as np
import jax
import jax.numpy as jnp
from jax import lax
from jax.experimental import pallas as pl
from jax.experimental.pallas import tpu as pltpu

BF = jnp.bfloat16
F32 = jnp.float32

GRID_W = 64
HEAD_DIM = 64
BRANCH = 512
NA_HEADS = 8
NA_ROWS = 8
NA_COLS = 16
NA_QROWS = 4
NA_WIN_ROWS = NA_QROWS + NA_ROWS
SW_Q_HEADS = 8
SW_KV_HEADS = 2
SW_GROUP = SW_Q_HEADS // SW_KV_HEADS
SW_BLOCK = 128
N_EXPERTS = 16
CAPACITY_FACTOR = 2
ROPE_BASE = 10000.0
RMS_EPS = 1e-6
NEG = -1e30
QK_SCALE = HEAD_DIM ** -0.5

LANES = 128
V7X_VMEM_BYTES = 64 * 1024 * 1024

C_AB, C_AC, C_AX = 0, 512, 1024
C_NAQ, C_NAK, C_NAV = 1536, 2048, 2560
C_SWQ, C_SWK, C_SWV = 3072, 3584, 3840
C_GATE, C_END = 4096, 7168


def _cparams(n_axes, vmem_mb):
    return pltpu.CompilerParams(dimension_semantics=("arbitrary",) * n_axes,
                                vmem_limit_bytes=vmem_mb * 1024 * 1024)


def _nt(a, b):
    return lax.dot_general(a, b, (((1,), (1,)), ((), ())), preferred_element_type=F32)


def _mm(a, b):
    return jnp.dot(a, b, preferred_element_type=F32)


def _half_mask(width):
    lane = lax.broadcasted_iota(jnp.int32, (1, width), 1)
    return (lane & (LANES - 1)) < HEAD_DIM


def _ada_kernel(cs_ref, w_ref, b_ref, o_ref):
    c = cs_ref[...]
    s = (c * jax.nn.sigmoid(c)).astype(BF)
    o_ref[0] = _mm(s, w_ref[0]) + b_ref[0]


def _ada_call(cs, w_ada_bf, b_ada):
    depth, d, n6 = w_ada_bf.shape
    rows = cs.shape[0]
    tn = 1536
    return pl.pallas_call(
        _ada_kernel,
        grid=(depth, n6 // tn),
        in_specs=[pl.BlockSpec((rows, d), lambda l, j: (0, 0)),
                  pl.BlockSpec((1, d, tn), lambda l, j: (l, 0, j)),
                  pl.BlockSpec((1, 1, tn), lambda l, j: (l, 0, j))],
        out_specs=pl.BlockSpec((1, rows, tn), lambda l, j: (l, 0, j)),
        out_shape=jax.ShapeDtypeStruct((depth, rows, n6), F32),
        compiler_params=_cparams(2, 32),
        name="adaln_mod",
    )(cs, w_ada_bf, b_ada.reshape(depth, 1, n6))


def _rope(z, cos, sin_signed):
    w = z.shape[1]
    rep = w // LANES
    cosw = jnp.concatenate([cos] * rep, axis=1) if rep > 1 else cos
    sinw = jnp.concatenate([sin_signed] * rep, axis=1) if rep > 1 else sin_signed
    lane = lax.broadcasted_iota(jnp.int32, (1, w), 1)
    first = (lane & (HEAD_DIM - 1)) < HEAD_DIM // 2
    partner = jnp.where(first, pltpu.roll(z, w - HEAD_DIM // 2, 1), pltpu.roll(z, HEAD_DIM // 2, 1))
    return z * cosw + partner * sinw


def _inproj_kernel(*refs, has_y, rope):
    it = iter(refs)
    x_ref = next(it)
    if has_y:
        y_ref, modp_ref = next(it), next(it)
    mod_ref, g_ref, w_ref = next(it), next(it), next(it)
    if rope:
        cos_ref, sin_ref = next(it), next(it)
    if has_y:
        xnew_ref = next(it)
    ab_ref, u_ref, naq_ref, nak_ref, nav_ref, swq_ref, swk_ref, swv_ref, gate_ref = it

    x = x_ref[0]
    if has_y:
        x = x + modp_ref[0][5:6, :] * y_ref[0]
        xnew_ref[0] = x
    m = mod_ref[0]
    ms = jnp.mean(x * x, axis=-1, keepdims=True)
    h = ((x * lax.rsqrt(ms + RMS_EPS)) * g_ref[...]) * (1.0 + m[1:2, :]) + m[0:1, :]
    hb = h.astype(BF)

    def proj(lo, hi):
        return _mm(hb, w_ref[:, lo:hi])

    ab_ref[0] = proj(C_AB, C_AC)
    u_ref[0] = proj(C_AC, C_AX) * proj(C_AX, C_NAQ)
    naq_ref[0] = (proj(C_NAQ, C_NAK) * QK_SCALE).astype(BF)
    nak_ref[0] = proj(C_NAK, C_NAV).astype(BF)
    nav_ref[0] = proj(C_NAV, C_SWQ).astype(BF)
    q = proj(C_SWQ, C_SWK)
    k = proj(C_SWK, C_SWV)
    if rope:
        cos, sin = cos_ref[...], sin_ref[...]
        q = _rope(q, cos, sin)
        k = _rope(k, cos, sin)
    swq_ref[0] = (q * QK_SCALE).astype(BF)
    swk_ref[0] = k.astype(BF)
    swv_ref[0] = proj(C_SWV, C_GATE).astype(BF)
    gate_ref[0] = jax.nn.sigmoid(proj(C_GATE, C_END)).astype(BF)


def _inproj_call(x, y, modp, mod, mod_row, g, w_ext, cos, sin, *, tm):
    b, n, d = x.shape
    has_y = y is not None
    rope = cos is not None
    nt = n // tm
    tok = lambda w: pl.BlockSpec((1, tm, w), lambda i, t: (i, t, 0))
    if mod_row is None:
        modspec = pl.BlockSpec((1, 6, d), lambda i, t: (i, 0, 0))
    else:
        modspec = pl.BlockSpec((1, 6, d), lambda i, t: (mod_row, 0, 0))
    args, in_specs = [x], [tok(d)]
    if has_y:
        args += [y, modp]
        in_specs += [tok(d), modspec]
    args += [mod, g.reshape(1, d), w_ext]
    in_specs += [modspec, pl.BlockSpec((1, d), lambda i, t: (0, 0)),
                 pl.BlockSpec((d, C_END), lambda i, t: (0, 0))]
    if rope:
        args += [cos, sin]
        in_specs += [pl.BlockSpec((tm, LANES), lambda i, t: (t, 0))] * 2
    widths = [(BRANCH, F32), (BRANCH, F32), (BRANCH, BF), (BRANCH, BF), (BRANCH, BF),
              (BRANCH, BF), (2 * LANES, BF), (2 * LANES, BF), (3 * d, BF)]
    out_shape = [jax.ShapeDtypeStruct((b, n, w), dt) for w, dt in widths]
    out_specs = [tok(w) for w, _ in widths]
    if has_y:
        out_shape = [jax.ShapeDtypeStruct((b, n, d), F32)] + out_shape
        out_specs = [tok(d)] + out_specs
    outs = pl.pallas_call(
        functools.partial(_inproj_kernel, has_y=has_y, rope=rope),
        grid=(b, nt), in_specs=in_specs, out_specs=out_specs, out_shape=out_shape,
        compiler_params=_cparams(2, 56),
        name="inproj",
    )(*args)
    if has_y:
        return outs[0], outs[1:]
    return x, outs


def _conv_kernel(ab_ref, u_ref, w_ref, o_ref):
    u = u_ref[0]
    n = u.shape[0]
    t = lax.broadcasted_iota(jnp.int32, (n, 1), 0)
    prev = jnp.where(t == 0, 0.0, pltpu.roll(u, 1, 0))
    nxt = jnp.where(t == n - 1, 0.0, pltpu.roll(u, n - 1, 0))
    w = w_ref[...]
    o_ref[0] = (ab_ref[0] * (w[0:1, :] * prev + w[1:2, :] * u + w[2:3, :] * nxt)).astype(BF)


def _conv_call(ab, u, w):
    b, n, c = ab.shape
    blk = pl.BlockSpec((1, n, c), lambda i: (i, 0, 0))
    return pl.pallas_call(
        _conv_kernel, grid=(b,),
        in_specs=[blk, blk, pl.BlockSpec((3, c), lambda i: (0, 0))],
        out_specs=blk, out_shape=jax.ShapeDtypeStruct((b, n, c), BF),
        compiler_params=_cparams(1, 48),
        name="short_conv",
    )(ab, u, w)


def _na_layout(rows):
    nblk = rows // NA_QROWS
    first, tiles = [], []
    for i in range(nblk):
        ks = int(np.clip(NA_QROWS * i - NA_ROWS // 2, 0, rows - NA_WIN_ROWS))
        tile = np.full((NA_QROWS, NA_WIN_ROWS), 2 * NA_ROWS - 1, np.int32)
        for rq in range(NA_QROWS):
            r = NA_QROWS * i + rq
            rs = int(np.clip(r - NA_ROWS // 2, 0, rows - NA_ROWS))
            for rk in range(NA_WIN_ROWS):
                ra = ks + rk
                if rs <= ra < rs + NA_ROWS:
                    tile[rq, rk] = ra - r + NA_ROWS - 1
        first.append(ks)
        tiles.append(tile)
    classes, cls_of = [], []
    for tle in tiles:
        for ci, c in enumerate(classes):
            if np.array_equal(c, tle):
                cls_of.append(ci)
                break
        else:
            classes.append(tle)
            cls_of.append(len(classes) - 1)
    return np.array(first, np.int32), np.array(cls_of, np.int32), np.stack(classes)


def _na_bias(na_rpb, cls_tiles):
    depth, nh = na_rpb.shape[:2]
    qc = np.arange(GRID_W)[:, None]
    kc = np.arange(GRID_W)[None, :]
    toe = np.clip(kc - qc + NA_COLS - 1, 0, 2 * NA_COLS - 2)
    cstart = np.clip(qc - NA_COLS // 2, 0, GRID_W - NA_COLS)
    col_ok = (kc >= cstart) & (kc < cstart + NA_COLS)
    m = jnp.take(na_rpb, jnp.asarray(toe.reshape(-1)), axis=-1)
    m = m.reshape(depth, nh, 2 * NA_ROWS - 1, GRID_W, GRID_W)
    m = jnp.where(jnp.asarray(col_ok), m, NEG)
    m = jnp.concatenate([m, jnp.full((depth, nh, 1, GRID_W, GRID_W), NEG, F32)], axis=2)
    ncls = cls_tiles.shape[0]
    b = jnp.take(m, jnp.asarray(cls_tiles.reshape(-1)), axis=2)
    b = b.reshape(depth, nh, ncls, NA_QROWS, NA_WIN_ROWS, GRID_W, GRID_W)
    b = b.transpose(0, 1, 2, 3, 5, 4, 6)
    b = b.reshape(depth, nh // 2, 2, ncls, NA_QROWS * GRID_W, NA_WIN_ROWS * GRID_W)
    return b.transpose(0, 1, 3, 2, 4, 5)


def _na_kernel(first_ref, cls_ref, q_ref, k_ref, v_ref, kc_ref, vc_ref, bias_ref, o_ref):
    i = pl.program_id(2)
    k0 = pl.multiple_of(first_ref[i] * GRID_W, GRID_W)
    cls = cls_ref[i]
    nwin = NA_WIN_ROWS * GRID_W
    q = q_ref[0]
    kw = k_ref[0, pl.ds(k0, nwin), :]
    vw = v_ref[0, pl.ds(k0, nwin), :]
    kc = kc_ref[0]
    vc = vc_ref[0]
    first = _half_mask(LANES)
    acc = jnp.zeros(q.shape, F32)
    for a in range(2):
        sel = first if a == 0 else jnp.logical_not(first)
        qa = jnp.where(sel, q, jnp.zeros_like(q))
        s_loc = _nt(qa, kw) + bias_ref[0, cls, a]
        s_ctx = _nt(qa, kc)
        m = jnp.maximum(jnp.max(s_loc, axis=-1, keepdims=True), jnp.max(s_ctx, axis=-1, keepdims=True))
        p_loc = jnp.exp(s_loc - m)
        p_ctx = jnp.exp(s_ctx - m)
        l = jnp.sum(p_loc, axis=-1, keepdims=True) + jnp.sum(p_ctx, axis=-1, keepdims=True)
        va = jnp.where(sel, vw, jnp.zeros_like(vw))
        vca = jnp.where(sel, vc, jnp.zeros_like(vc))
        o = _mm(p_loc.astype(BF), va) + _mm(p_ctx.astype(BF), vca)
        acc = acc + o / l
    o_ref[0] = acc.astype(BF)


def _na_call(q, k, v, kc, vc, bias, first, cls_of):
    b, s, w = q.shape
    l = kc.shape[1]
    npair = w // LANES
    tq = NA_QROWS * GRID_W
    nblk = s // tq
    ncls = bias.shape[1]
    grid_spec = pltpu.PrefetchScalarGridSpec(
        num_scalar_prefetch=2, grid=(npair, b, nblk),
        in_specs=[pl.BlockSpec((1, tq, LANES), lambda p, i, t, *_: (i, t, p)),
                  pl.BlockSpec((1, s, LANES), lambda p, i, t, *_: (i, 0, p)),
                  pl.BlockSpec((1, s, LANES), lambda p, i, t, *_: (i, 0, p)),
                  pl.BlockSpec((1, l, LANES), lambda p, i, t, *_: (i, 0, p)),
                  pl.BlockSpec((1, l, LANES), lambda p, i, t, *_: (i, 0, p)),
                  pl.BlockSpec((1, ncls, 2, tq, NA_WIN_ROWS * GRID_W), lambda p, i, t, *_: (p, 0, 0, 0, 0))],
        out_specs=pl.BlockSpec((1, tq, LANES), lambda p, i, t, *_: (i, t, p)))
    return pl.pallas_call(
        _na_kernel, grid_spec=grid_spec, out_shape=jax.ShapeDtypeStruct((b, s, w), BF),
        compiler_params=_cparams(3, 48),
        name="nbr_attn",
    )(jnp.asarray(first), jnp.asarray(cls_of), q, k, v, kc, vc, bias)


def _stack_heads(q4):
    first = _half_mask(LANES)
    parts = []
    for g in range(SW_GROUP):
        pair = q4[:, (g // 2) * LANES:(g // 2 + 1) * LANES]
        sel = first if g % 2 == 0 else jnp.logical_not(first)
        parts.append(jnp.where(sel, pair, jnp.zeros_like(pair)))
    return jnp.concatenate(parts, axis=0)


def _unstack_heads(o, n):
    first = _half_mask(LANES)
    pairs = [jnp.where(first, o[(2 * j) * n:(2 * j + 1) * n], o[(2 * j + 1) * n:(2 * j + 2) * n])
             for j in range(SW_GROUP // 2)]
    return jnp.concatenate(pairs, axis=1)


def _sink_column(sink_ref, base, n):
    return jnp.concatenate([jnp.full((n, 1), sink_ref[base + g], F32) for g in range(SW_GROUP)], axis=0)


def _sw_kernel(sink_ref, q_ref, k_ref, v_ref, kc_ref, vc_ref, o_ref, *, nb):
    kvh = pl.program_id(1)
    n = pl.program_id(2)
    tq = SW_BLOCK
    qs = _stack_heads(q_ref[0])
    row = lax.broadcasted_iota(jnp.int32, (SW_GROUP * tq, tq), 0) & (tq - 1)
    col = lax.broadcasted_iota(jnp.int32, (SW_GROUP * tq, tq), 1)
    scores, values = [], []
    for c in (-1, 0, 1):
        blk = jnp.clip(n + c, 0, nb - 1)
        off = pl.multiple_of(blk * tq, tq)
        s = _nt(qs, k_ref[0, pl.ds(off, tq), :])
        if c != 0:
            inside = jnp.logical_and(n + c >= 0, n + c <= nb - 1)
            band = (col >= row) if c < 0 else (col <= row)
            s = jnp.where(jnp.logical_and(band, inside), s, NEG)
        scores.append(s)
        values.append(v_ref[0, pl.ds(off, tq), :])
    scores.append(_nt(qs, kc_ref[0]))
    values.append(vc_ref[0])
    sink = _sink_column(sink_ref, kvh * SW_GROUP, tq)
    m = sink
    for s in scores:
        m = jnp.maximum(m, jnp.max(s, axis=-1, keepdims=True))
    l = jnp.exp(sink - m)
    o = jnp.zeros((SW_GROUP * tq, LANES), F32)
    for s, v in zip(scores, values):
        p = jnp.exp(s - m)
        l = l + jnp.sum(p, axis=-1, keepdims=True)
        o = o + _mm(p.astype(BF), v)
    o_ref[0] = _unstack_heads(o / l, tq).astype(BF)


def _sw_call(q, k, v, kc, vc, sink):
    b, s, w = q.shape
    l = kc.shape[1]
    nb = s // SW_BLOCK
    gw = SW_GROUP * HEAD_DIM
    return pl.pallas_call(
        functools.partial(_sw_kernel, nb=nb),
        grid=(b, SW_KV_HEADS, nb),
        in_specs=[pl.BlockSpec(memory_space=pltpu.SMEM),
                  pl.BlockSpec((1, SW_BLOCK, gw), lambda i, h, t: (i, t, h)),
                  pl.BlockSpec((1, s, LANES), lambda i, h, t: (i, 0, h)),
                  pl.BlockSpec((1, s, LANES), lambda i, h, t: (i, 0, h)),
                  pl.BlockSpec((1, l, LANES), lambda i, h, t: (i, 0, h)),
                  pl.BlockSpec((1, l, LANES), lambda i, h, t: (i, 0, h))],
        out_specs=pl.BlockSpec((1, SW_BLOCK, gw), lambda i, h, t: (i, t, h)),
        out_shape=jax.ShapeDtypeStruct((b, s, w), BF),
        compiler_params=_cparams(3, 32),
        name="window_attn",
    )(sink, q, k, v, kc, vc)


def _ctx_attn_kernel(sink_ref, naq_ref, nak_ref, nav_ref, swq_ref, swk_ref, swv_ref, ob_ref, oc_ref):
    first = _half_mask(LANES)
    n = naq_ref.shape[1]
    for p in range(NA_HEADS // 2):
        lo, hi = p * LANES, (p + 1) * LANES
        q, k, v = naq_ref[0, :, lo:hi], nak_ref[0, :, lo:hi], nav_ref[0, :, lo:hi]
        acc = jnp.zeros(q.shape, F32)
        for a in range(2):
            sel = first if a == 0 else jnp.logical_not(first)
            s = _nt(jnp.where(sel, q, jnp.zeros_like(q)), k)
            m = jnp.max(s, axis=-1, keepdims=True)
            pr = jnp.exp(s - m)
            l = jnp.sum(pr, axis=-1, keepdims=True)
            acc = acc + _mm(pr.astype(BF), jnp.where(sel, v, jnp.zeros_like(v))) / l
        ob_ref[0, :, lo:hi] = acc.astype(BF)
    gw = SW_GROUP * HEAD_DIM
    for h in range(SW_KV_HEADS):
        qs = _stack_heads(swq_ref[0, :, h * gw:(h + 1) * gw])
        s = _nt(qs, swk_ref[0, :, h * LANES:(h + 1) * LANES])
        sink = _sink_column(sink_ref, h * SW_GROUP, n)
        m = jnp.maximum(sink, jnp.max(s, axis=-1, keepdims=True))
        pr = jnp.exp(s - m)
        l = jnp.exp(sink - m) + jnp.sum(pr, axis=-1, keepdims=True)
        o = _mm(pr.astype(BF), swv_ref[0, :, h * LANES:(h + 1) * LANES]) / l
        oc_ref[0, :, h * gw:(h + 1) * gw] = _unstack_heads(o, n).astype(BF)


def _ctx_attn_call(naq, nak, nav, swq, swk, swv, sink):
    b, l, w = naq.shape
    blk = lambda width: pl.BlockSpec((1, l, width), lambda i: (i, 0, 0))
    return pl.pallas_call(
        _ctx_attn_kernel, grid=(b,),
        in_specs=[pl.BlockSpec(memory_space=pltpu.SMEM), blk(w), blk(w), blk(w),
                  blk(w), blk(2 * LANES), blk(2 * LANES)],
        out_specs=[blk(w), blk(w)],
        out_shape=[jax.ShapeDtypeStruct((b, l, w), BF)] * 2,
        compiler_params=_cparams(1, 32),
        name="ctx_attn",
    )(sink, naq, nak, nav, swq, swk, swv)


def _merge_kernel(oa_ref, ob_ref, oc_ref, gate_ref, x_ref, mod_ref, g_ref, wb_ref, wo_ref, wr_ref,
                  x1_ref, h2_ref, aff_ref):
    d = x_ref.shape[2]
    merged = None
    for j, o_ref in enumerate((oa_ref, ob_ref, oc_ref)):
        term = gate_ref[0, :, j * d:(j + 1) * d].astype(F32) * _mm(o_ref[0], wb_ref[j])
        merged = term if merged is None else merged + term
    m = mod_ref[0]
    x1 = x_ref[0] + m[2:3, :] * _mm(merged.astype(BF), wo_ref[...])
    x1_ref[0] = x1
    ms = jnp.mean(x1 * x1, axis=-1, keepdims=True)
    h2 = ((x1 * lax.rsqrt(ms + RMS_EPS)) * g_ref[...]) * (1.0 + m[4:5, :]) + m[3:4, :]
    h2b = h2.astype(BF)
    h2_ref[0] = h2b
    logits = _nt(wr_ref[...], h2b)
    e = jnp.exp(logits - jnp.max(logits, axis=0, keepdims=True))
    aff_ref[0] = e / jnp.sum(e, axis=0, keepdims=True)


def _merge_call(oa, ob, oc, gates, x, mod, mod_row, g, wb, wo, wr_t, *, tm):
    b, n, d = x.shape
    tok = lambda w: pl.BlockSpec((1, tm, w), lambda i, t: (i, t, 0))
    if mod_row is None:
        modspec = pl.BlockSpec((1, 6, d), lambda i, t: (i, 0, 0))
    else:
        modspec = pl.BlockSpec((1, 6, d), lambda i, t: (mod_row, 0, 0))
    return pl.pallas_call(
        _merge_kernel, grid=(b, n // tm),
        in_specs=[tok(BRANCH), tok(BRANCH), tok(BRANCH), tok(3 * d), tok(d), modspec,
                  pl.BlockSpec((1, d), lambda i, t: (0, 0)),
                  pl.BlockSpec((3, BRANCH, d), lambda i, t: (0, 0, 0)),
                  pl.BlockSpec((d, d), lambda i, t: (0, 0)),
                  pl.BlockSpec((N_EXPERTS, d), lambda i, t: (0, 0))],
        out_specs=[tok(d), tok(d), pl.BlockSpec((1, N_EXPERTS, tm), lambda i, t: (i, 0, t))],
        out_shape=[jax.ShapeDtypeStruct((b, n, d), F32), jax.ShapeDtypeStruct((b, n, d), BF),
                   jax.ShapeDtypeStruct((b, N_EXPERTS, n), F32)],
        compiler_params=_cparams(2, 48),
        name="merge_norm_router",
    )(oa, ob, oc, gates, x, mod, g.reshape(1, d), wb, wo, wr_t)


def _route_consts(nch, seg_rows, cap):
    r = N_EXPERTS * nch
    idx = np.arange(r)
    seg = idx // seg_rows
    same = seg[:, None] == seg[None, :]
    before = same & (idx[None, :] < idx[:, None])
    lane = np.arange(LANES)
    upper = (lane[:, None] < lane[None, :])
    base = ((idx % nch) // seg_rows * cap).astype(np.float32).reshape(r, 1)
    return (jnp.asarray(same, BF), jnp.asarray(before, BF), jnp.asarray(upper, BF), jnp.asarray(base))


def _route_kernel(aff_ref, same_ref, before_ref, upper_ref, base_ref, pos_ref, *, cap):
    aff = aff_ref[0]
    bits = lax.bitcast_convert_type(aff, jnp.int32)
    same, before, upper = same_ref[...], before_ref[...], upper_ref[...]
    r = aff.shape[0]

    def seg_total(mask_bf):
        return jnp.sum(_mm(same, mask_bf), axis=1, keepdims=True)

    def prefix(mask_bf):
        return _mm(mask_bf, upper) + jnp.sum(_mm(before, mask_bf), axis=1, keepdims=True)

    def body(i, thr):
        cand = thr | lax.shift_left(jnp.int32(1), 30 - i)
        ge = jnp.where(bits >= cand, 1.0, 0.0).astype(BF)
        return jnp.where(seg_total(ge) >= cap, cand, thr)

    thr = lax.fori_loop(0, 31, body, jnp.zeros((r, 1), jnp.int32))
    gt = bits > thr
    eq = bits == thr
    need = cap - seg_total(jnp.where(gt, 1.0, 0.0).astype(BF))
    eq_rank = prefix(jnp.where(eq, 1.0, 0.0).astype(BF))
    sel = jnp.logical_or(gt, jnp.logical_and(eq, eq_rank < need))
    slot = prefix(jnp.where(sel, 1.0, 0.0).astype(BF)) + base_ref[...]
    pos_ref[0] = jnp.where(sel, slot, -1.0)


def _route_call(aff_rows, nch, seg_rows, cap):
    g, r, _ = aff_rows.shape
    consts = _route_consts(nch, seg_rows, cap)
    full = lambda a: pl.BlockSpec(a.shape, lambda i: (0,) * a.ndim)
    return pl.pallas_call(
        functools.partial(_route_kernel, cap=cap), grid=(g,),
        in_specs=[pl.BlockSpec((1, r, LANES), lambda i: (i, 0, 0))] + [full(a) for a in consts],
        out_specs=pl.BlockSpec((1, r, LANES), lambda i: (i, 0, 0)),
        out_shape=jax.ShapeDtypeStruct((g, r, LANES), F32),
        compiler_params=_cparams(1, 32),
        name="expert_choice_select",
    )(aff_rows, *consts)


def _moe_kernel(h_ref, pos_ref, post_ref, aff_ref, wg_ref, wu_ref, wd_ref, y_ref, *, nslots, rows_per_store):
    e = pl.program_id(1)

    @pl.when(e == 0)
    def _():
        y_ref[...] = jnp.zeros_like(y_ref)

    pos = pos_ref[0, 0]
    aff = aff_ref[0, 0]
    nch = pos.shape[0]
    slot_col = lax.broadcasted_iota(jnp.int32, (nslots, LANES), 0).astype(F32)
    onehot, gate = [], jnp.zeros((nslots, 1), F32)
    for c in range(nch):
        hit = slot_col == pos[c:c + 1, :]
        onehot.append(jnp.where(hit, 1.0, 0.0).astype(BF))
        gate = gate + jnp.sum(jnp.where(hit, aff[c:c + 1, :], 0.0), axis=1, keepdims=True)
    xe = _mm(jnp.concatenate(onehot, axis=1), h_ref[0]).astype(BF)
    a = _mm(xe, wg_ref[0])
    u = _mm(xe, wu_ref[0])
    mid = ((a * jax.nn.sigmoid(a)) * u).astype(BF)
    ye = (_mm(mid, wd_ref[0]) * gate).astype(BF)

    post = post_ref[0, 0]
    slot_row = lax.broadcasted_iota(jnp.int32, (LANES, nslots), 1).astype(F32)
    per = rows_per_store // LANES
    for c0 in range(0, nch, per):
        scat = jnp.concatenate(
            [jnp.where(post[:, c:c + 1] == slot_row, 1.0, 0.0).astype(BF) for c in range(c0, min(c0 + per, nch))],
            axis=0)
        lo = c0 * LANES
        y_ref[0, lo:lo + scat.shape[0], :] += _mm(scat, ye)


def _moe_call(h2, pos, post, aff, wg, wu, wd, nslots):
    g, n, d = h2.shape
    nch = n // LANES
    ff = wg.shape[2]
    tile = lambda: pl.BlockSpec((1, 1, nch, LANES), lambda i, e: (i, e, 0, 0))
    return pl.pallas_call(
        functools.partial(_moe_kernel, nslots=nslots, rows_per_store=min(n, 512)),
        grid=(g, N_EXPERTS),
        in_specs=[pl.BlockSpec((1, n, d), lambda i, e: (i, 0, 0)),
                  tile(), pl.BlockSpec((1, 1, LANES, nch), lambda i, e: (i, e, 0, 0)), tile(),
                  pl.BlockSpec((1, d, ff), lambda i, e: (e, 0, 0)),
                  pl.BlockSpec((1, d, ff), lambda i, e: (e, 0, 0)),
                  pl.BlockSpec((1, ff, d), lambda i, e: (e, 0, 0))],
        out_specs=pl.BlockSpec((1, n, d), lambda i, e: (i, 0, 0)),
        out_shape=jax.ShapeDtypeStruct((g, n, d), F32),
        compiler_params=_cparams(2, 56),
        name="expert_ffn",
    )(h2, pos, post, aff, wg, wu, wd)


def _expert_choice(h2, aff, wg, wu, wd, group):
    b, n, d = h2.shape
    cap = CAPACITY_FACTOR * n // N_EXPERTS
    g = b // group
    ng = group * n
    nch = ng // LANES
    aff_g = aff.reshape(g, group, N_EXPERTS, n).transpose(0, 2, 1, 3).reshape(g, N_EXPERTS * nch, LANES)
    pos = _route_call(aff_g, nch, n // LANES, cap)
    pos4 = pos.reshape(g, N_EXPERTS, nch, LANES)
    y = _moe_call(h2.reshape(g, ng, d), pos4, pos4.transpose(0, 1, 3, 2),
                  aff_g.reshape(g, N_EXPERTS, nch, LANES), wg, wu, wd, group * cap)
    return y.reshape(b, n, d)


def _final_kernel(x_ref, y_ref, mod_ref, g_ref, o_ref):
    x = x_ref[0] + mod_ref[0][5:6, :] * y_ref[0]
    ms = jnp.mean(x * x, axis=-1, keepdims=True)
    o_ref[0] = (x * lax.rsqrt(ms + RMS_EPS)) * g_ref[...]


def _final_call(x, y, mod, g, *, tm):
    b, n, d = x.shape
    tok = pl.BlockSpec((1, tm, d), lambda i, t: (i, t, 0))
    return pl.pallas_call(
        _final_kernel, grid=(b, n // tm),
        in_specs=[tok, tok, pl.BlockSpec((1, 6, d), lambda i, t: (i, 0, 0)),
                  pl.BlockSpec((1, d), lambda i, t: (0, 0))],
        out_specs=tok, out_shape=jax.ShapeDtypeStruct((b, n, d), F32),
        compiler_params=_cparams(2, 32),
        name="final_norm",
    )(x, y, mod, g.reshape(1, d))


def _rope_tables(s):
    t = np.arange(s)
    row = (t // GRID_W).astype(np.float32)
    col = (t % GRID_W).astype(np.float32)
    nfreq = HEAD_DIM // 4
    inv = (ROPE_BASE ** (-jnp.arange(nfreq, dtype=F32) / nfreq))
    ang = jnp.concatenate([jnp.asarray(row)[:, None] * inv, jnp.asarray(col)[:, None] * inv], axis=-1)
    cos, sin = jnp.cos(ang), jnp.sin(ang)
    cosw = jnp.concatenate([cos, cos] * 2, axis=-1)
    sinw = jnp.concatenate([-sin, sin] * 2, axis=-1)
    return cosw, sinw


def _extend_w_in(w_in):
    k = w_in[..., C_SWK:C_SWK + SW_KV_HEADS * HEAD_DIM]
    v = w_in[..., C_SWK + SW_KV_HEADS * HEAD_DIM:C_SWK + 2 * SW_KV_HEADS * HEAD_DIM]
    dup = lambda a: jnp.concatenate([a[..., :HEAD_DIM]] * 2 + [a[..., HEAD_DIM:]] * 2, axis=-1)
    rest = w_in[..., C_SWK + 2 * SW_KV_HEADS * HEAD_DIM:]
    return jnp.concatenate([w_in[..., :C_SWK], dup(k), dup(v), rest], axis=-1).astype(BF)


def _ctx_group(b, l):
    g = max(1, min(b, 2048 // l))
    while b % g:
        g -= 1
    return g


def kernel(x, c, ctx, c_ctx, w_ada, b_ada, g_mix, w_in, conv_w, na_rpb, sw_sink, w_branch, w_out,
           g_ffn, w_router, w_exp_gate, w_exp_up, w_exp_down, g_final):
    bsz, s, d = x.shape
    l = ctx.shape[1]
    depth = w_ada.shape[0]
    assert d == 1024 and s % (NA_QROWS * GRID_W) == 0 and s // GRID_W >= NA_WIN_ROWS and l % LANES == 0
    tm = 256
    tmc = min(256, l)

    mod_rows = -(-(bsz + 1) // 8) * 8
    cs = jnp.zeros((mod_rows, d), F32).at[:bsz].set(c).at[bsz].set(c_ctx)
    mod = _ada_call(cs, w_ada.astype(BF), b_ada).reshape(depth, mod_rows, 6, d)

    w_ext = _extend_w_in(w_in)
    wb, wo = w_branch.astype(BF), w_out.astype(BF)
    wr_t = jnp.swapaxes(w_router, 1, 2).astype(BF)
    wg, wu, wd = w_exp_gate.astype(BF), w_exp_up.astype(BF), w_exp_down.astype(BF)
    cosw, sinw = _rope_tables(s)
    first, cls_of, cls_tiles = _na_layout(s // GRID_W)
    bias = _na_bias(na_rpb, cls_tiles)
    cgroup = _ctx_group(bsz, l)

    xc = ctx
    y = yc = None
    for li in range(depth):
        last = li == depth - 1
        modp = mod[li - 1] if li else None
        x, (ab, u, naq, nak, nav, swq, swk, swv, gates) = _inproj_call(
            x, y, modp, mod[li], None, g_mix[li], w_ext[li], cosw, sinw, tm=tm)
        xc, (cab, cu, cnaq, cnak, cnav, cswq, cswk, cswv, cgates) = _inproj_call(
            xc, yc, modp, mod[li], bsz, g_mix[li], w_ext[li], None, None, tm=tmc)

        o_a = _conv_call(ab, u, conv_w[li])
        o_b = _na_call(naq, nak, nav, cnak, cnav, bias[li], first, cls_of)
        o_c = _sw_call(swq, swk, swv, cswk, cswv, sw_sink[li])
        x, h2, aff = _merge_call(o_a, o_b, o_c, gates, x, mod[li], None, g_ffn[li], wb[li], wo[li], wr_t[li], tm=tm)
        y = _expert_choice(h2, aff, wg[li], wu[li], wd[li], 1)
        if not last:
            co_a = _conv_call(cab, cu, conv_w[li])
            co_b, co_c = _ctx_attn_call(cnaq, cnak, cnav, cswq, cswk, cswv, sw_sink[li])
            xc, h2c, affc = _merge_call(co_a, co_b, co_c, cgates, xc, mod[li], bsz, g_ffn[li],
                                        wb[li], wo[li], wr_t[li], tm=tmc)
            yc = _expert_choice(h2c, affc, wg[li], wu[li], wd[li], cgroup)
    return _final_call(x, y, mod[depth - 1], g_final, tm=tm)
```

```python
import functools

import numpy as np
import jax
import jax.numpy as jnp
from jax import lax
from jax.experimental import pallas as pl
from jax.experimental.pallas import tpu as pltpu

BF = jnp.bfloat16
F32 = jnp.float32

GRID_W = 64
HEAD_DIM = 64
BRANCH = 512
NA_HEADS = 8
NA_ROWS = 8
NA_COLS = 16
NA_QROWS = 4
NA_WIN_ROWS = NA_QROWS + NA_ROWS
SW_Q_HEADS = 8
SW_KV_HEADS = 2
SW_GROUP = SW_Q_HEADS // SW_KV_HEADS
SW_BLOCK = 128
N_EXPERTS = 16
CAPACITY_FACTOR = 2
ROPE_BASE = 10000.0
RMS_EPS = 1e-6
NEG = -1e30
QK_SCALE = HEAD_DIM ** -0.5

LANES = 128
V7X_VMEM_BYTES = 64 * 1024 * 1024

C_AB, C_AC, C_AX = 0, 512, 1024
C_NAQ, C_NAK, C_NAV = 1536, 2048, 2560
C_SWQ, C_SWK, C_SWV = 3072, 3584, 3840
C_GATE, C_END = 4096, 7168


def _cparams(n_axes, vmem_mb):
    return pltpu.CompilerParams(dimension_semantics=("arbitrary",) * n_axes,
                                vmem_limit_bytes=vmem_mb * 1024 * 1024)


def _nt(a, b):
    return lax.dot_general(a, b, (((1,), (1,)), ((), ())), preferred_element_type=F32)


def _mm(a, b):
    return jnp.dot(a, b, preferred_element_type=F32)


def _half_mask(width):
    lane = lax.broadcasted_iota(jnp.int32, (1, width), 1)
    return (lane & (LANES - 1)) < HEAD_DIM


def _lane_blocks(s):
    return [s[:, j:j + LANES] for j in range(0, s.shape[1], LANES)]


def _softmax_pv(scores, values, sink=None):
    blocks = [blk for s in scores for blk in _lane_blocks(s)]
    mv = blocks[0]
    for blk in blocks[1:]:
        mv = jnp.maximum(mv, blk)
    m = jnp.max(mv, axis=-1, keepdims=True)
    if sink is not None:
        m = jnp.maximum(m, sink)
    lv, o = None, None
    for s, v in zip(scores, values):
        p = jnp.exp(s - m)
        for blk in _lane_blocks(p):
            lv = blk if lv is None else lv + blk
        t = _mm(p.astype(BF), v)
        o = t if o is None else o + t
    l = jnp.sum(lv, axis=-1, keepdims=True)
    if sink is not None:
        l = l + jnp.exp(sink - m)
    return o * (1.0 / l)


def _ada_kernel(cs_ref, w_ref, b_ref, o_ref):
    c = cs_ref[...]
    s = (c * jax.nn.sigmoid(c)).astype(BF)
    o_ref[0] = _mm(s, w_ref[0]) + b_ref[0]


def _ada_call(cs, w_ada_bf, b_ada):
    depth, d, n6 = w_ada_bf.shape
    rows = cs.shape[0]
    tn = 1536
    return pl.pallas_call(
        _ada_kernel,
        grid=(depth, n6 // tn),
        in_specs=[pl.BlockSpec((rows, d), lambda l, j: (0, 0)),
                  pl.BlockSpec((1, d, tn), lambda l, j: (l, 0, j)),
                  pl.BlockSpec((1, 1, tn), lambda l, j: (l, 0, j))],
        out_specs=pl.BlockSpec((1, rows, tn), lambda l, j: (l, 0, j)),
        out_shape=jax.ShapeDtypeStruct((depth, rows, n6), F32),
        compiler_params=_cparams(2, 32),
        name="adaln_mod",
    )(cs, w_ada_bf, b_ada.reshape(depth, 1, n6))


def _rope(z, cos, sin_signed):
    w = z.shape[1]
    rep = w // LANES
    cosw = jnp.concatenate([cos] * rep, axis=1) if rep > 1 else cos
    sinw = jnp.concatenate([sin_signed] * rep, axis=1) if rep > 1 else sin_signed
    lane = lax.broadcasted_iota(jnp.int32, (1, w), 1)
    first = (lane & (HEAD_DIM - 1)) < HEAD_DIM // 2
    partner = jnp.where(first, pltpu.roll(z, w - HEAD_DIM // 2, 1), pltpu.roll(z, HEAD_DIM // 2, 1))
    return z * cosw + partner * sinw


def _inproj_kernel(*refs, has_y, rope):
    it = iter(refs)
    x_ref = next(it)
    if has_y:
        y_ref, modp_ref = next(it), next(it)
    mod_ref, g_ref, w_ref = next(it), next(it), next(it)
    if rope:
        cos_ref, sin_ref = next(it), next(it)
    if has_y:
        xnew_ref = next(it)
    ab_ref, u_ref, naq_ref, nak_ref, nav_ref, swq_ref, swk_ref, swv_ref, gate_ref = it

    x = x_ref[0]
    if has_y:
        x = x + modp_ref[0][5:6, :] * y_ref[0]
        xnew_ref[0] = x
    m = mod_ref[0]
    ms = jnp.mean(x * x, axis=-1, keepdims=True)
    h = ((x * lax.rsqrt(ms + RMS_EPS)) * g_ref[...]) * (1.0 + m[1:2, :]) + m[0:1, :]
    hb = h.astype(BF)

    def proj(lo, hi):
        return _mm(hb, w_ref[:, lo:hi])

    ab_ref[0] = proj(C_AB, C_AC)
    u_ref[0] = proj(C_AC, C_AX) * proj(C_AX, C_NAQ)
    naq_ref[0] = (proj(C_NAQ, C_NAK) * QK_SCALE).astype(BF)
    nak_ref[0] = proj(C_NAK, C_NAV).astype(BF)
    nav_ref[0] = proj(C_NAV, C_SWQ).astype(BF)
    q = proj(C_SWQ, C_SWK)
    k = proj(C_SWK, C_SWV)
    if rope:
        cos, sin = cos_ref[...], sin_ref[...]
        q = _rope(q, cos, sin)
        k = _rope(k, cos, sin)
    swq_ref[0] = (q * QK_SCALE).astype(BF)
    swk_ref[0] = k.astype(BF)
    swv_ref[0] = proj(C_SWV, C_GATE).astype(BF)
    gate_ref[0] = jax.nn.sigmoid(proj(C_GATE, C_END)).astype(BF)


def _mod_spec(layer, mod_row, d):
    if mod_row is None:
        return pl.BlockSpec((None, 1, 6, d), lambda i, t: (layer, i, 0, 0))
    return pl.BlockSpec((None, 1, 6, d), lambda i, t: (layer, mod_row, 0, 0))


def _layer_spec(arr, layer):
    zeros = (0,) * (arr.ndim - 1)
    return pl.BlockSpec((None,) + arr.shape[1:], lambda i, t: (layer,) + zeros, pipeline_mode=pl.Buffered(1))


def _inproj_call(x, y, mod, layer, mod_row, g, w_ext, cos, sin, *, tm):
    b, n, d = x.shape
    has_y = y is not None
    rope = cos is not None
    nt = n // tm
    tok = lambda w: pl.BlockSpec((1, tm, w), lambda i, t: (i, t, 0))
    args, in_specs = [x], [tok(d)]
    if has_y:
        args += [y, mod]
        in_specs += [tok(d), _mod_spec(layer - 1, mod_row, d)]
    args += [mod, g, w_ext]
    in_specs += [_mod_spec(layer, mod_row, d), _layer_spec(g, layer), _layer_spec(w_ext, layer)]
    if rope:
        args += [cos, sin]
        in_specs += [pl.BlockSpec((tm, LANES), lambda i, t: (t, 0))] * 2
    widths = [(BRANCH, F32), (BRANCH, F32), (BRANCH, BF), (BRANCH, BF), (BRANCH, BF),
              (BRANCH, BF), (2 * LANES, BF), (2 * LANES, BF), (3 * d, BF)]
    out_shape = [jax.ShapeDtypeStruct((b, n, w), dt) for w, dt in widths]
    out_specs = [tok(w) for w, _ in widths]
    if has_y:
        out_shape = [jax.ShapeDtypeStruct((b, n, d), F32)] + out_shape
        out_specs = [tok(d)] + out_specs
    outs = pl.pallas_call(
        functools.partial(_inproj_kernel, has_y=has_y, rope=rope),
        grid=(b, nt), in_specs=in_specs, out_specs=out_specs, out_shape=out_shape,
        compiler_params=_cparams(2, 56),
        name="inproj",
    )(*args)
    if has_y:
        return outs[0], outs[1:]
    return x, outs


def _conv_kernel(ab_ref, u_ref, w_ref, o_ref):
    u = u_ref[0]
    n = u.shape[0]
    t = lax.broadcasted_iota(jnp.int32, (n, 1), 0)
    prev = jnp.where(t == 0, 0.0, pltpu.roll(u, 1, 0))
    nxt = jnp.where(t == n - 1, 0.0, pltpu.roll(u, n - 1, 0))
    w = w_ref[...]
    o_ref[0] = (ab_ref[0] * (w[0:1, :] * prev + w[1:2, :] * u + w[2:3, :] * nxt)).astype(BF)


def _conv_call(ab, u, w, layer):
    b, n, c = ab.shape
    blk = pl.BlockSpec((1, n, c), lambda i: (i, 0, 0))
    return pl.pallas_call(
        _conv_kernel, grid=(b,),
        in_specs=[blk, blk, pl.BlockSpec((None, 3, c), lambda i: (layer, 0, 0))],
        out_specs=blk, out_shape=jax.ShapeDtypeStruct((b, n, c), BF),
        compiler_params=_cparams(1, 48),
        name="short_conv",
    )(ab, u, w)


def _na_layout(rows):
    nblk = rows // NA_QROWS
    first, tiles = [], []
    for i in range(nblk):
        ks = int(np.clip(NA_QROWS * i - NA_ROWS // 2, 0, rows - NA_WIN_ROWS))
        tile = np.full((NA_QROWS, NA_WIN_ROWS), 2 * NA_ROWS - 1, np.int32)
        for rq in range(NA_QROWS):
            r = NA_QROWS * i + rq
            rs = int(np.clip(r - NA_ROWS // 2, 0, rows - NA_ROWS))
            for rk in range(NA_WIN_ROWS):
                ra = ks + rk
                if rs <= ra < rs + NA_ROWS:
                    tile[rq, rk] = ra - r + NA_ROWS - 1
        first.append(ks)
        tiles.append(tile)
    classes, cls_of = [], []
    for tle in tiles:
        for ci, c in enumerate(classes):
            if np.array_equal(c, tle):
                cls_of.append(ci)
                break
        else:
            classes.append(tle)
            cls_of.append(len(classes) - 1)
    return np.array(first, np.int32), np.array(cls_of, np.int32), np.stack(classes)


def _na_bias(na_rpb, cls_tiles):
    depth, nh = na_rpb.shape[:2]
    qc = np.arange(GRID_W)[:, None]
    kc = np.arange(GRID_W)[None, :]
    toe = np.clip(kc - qc + NA_COLS - 1, 0, 2 * NA_COLS - 2)
    cstart = np.clip(qc - NA_COLS // 2, 0, GRID_W - NA_COLS)
    col_ok = (kc >= cstart) & (kc < cstart + NA_COLS)
    m = jnp.take(na_rpb, jnp.asarray(toe.reshape(-1)), axis=-1)
    m = m.reshape(depth, nh, 2 * NA_ROWS - 1, GRID_W, GRID_W)
    m = jnp.where(jnp.asarray(col_ok), m, NEG)
    m = jnp.concatenate([m, jnp.full((depth, nh, 1, GRID_W, GRID_W), NEG, F32)], axis=2)
    ncls = cls_tiles.shape[0]
    b = jnp.take(m, jnp.asarray(cls_tiles.reshape(-1)), axis=2)
    b = b.reshape(depth, nh, ncls, NA_QROWS, NA_WIN_ROWS, GRID_W, GRID_W)
    b = b.transpose(0, 1, 2, 3, 5, 4, 6)
    b = b.reshape(depth, nh // 2, 2, ncls, NA_QROWS * GRID_W, NA_WIN_ROWS * GRID_W)
    b = b.transpose(0, 1, 3, 2, 4, 5)
    return b.reshape(depth, nh // 2, ncls, 2 * NA_QROWS * GRID_W, NA_WIN_ROWS * GRID_W)


def _stack_pair(q):
    first = _half_mask(LANES)
    zero = jnp.zeros_like(q)
    return jnp.concatenate([jnp.where(first, q, zero), jnp.where(first, zero, q)], axis=0)


def _unstack_pair(o, n):
    return jnp.where(_half_mask(LANES), o[:n], o[n:])


def _na_kernel(first_ref, cls_ref, q_ref, k_ref, v_ref, kc_ref, vc_ref, bias_ref, o_ref, *, nblk):
    tq = NA_QROWS * GRID_W
    nwin = NA_WIN_ROWS * GRID_W
    kc = kc_ref[0]
    vc = vc_ref[0]

    def body(i, carry):
        k0 = pl.multiple_of(first_ref[i] * GRID_W, tq)
        q0 = pl.multiple_of(i * tq, tq)
        qs = _stack_pair(q_ref[0, pl.ds(q0, tq), :])
        s_loc = _nt(qs, k_ref[0, pl.ds(k0, nwin), :]) + bias_ref[cls_ref[i]]
        s_ctx = _nt(qs, kc)
        o = _softmax_pv([s_loc, s_ctx], [v_ref[0, pl.ds(k0, nwin), :], vc])
        o_ref[0, pl.ds(q0, tq), :] = _unstack_pair(o, tq).astype(BF)
        return carry

    lax.fori_loop(0, nblk, body, 0, unroll=2)


def _na_call(q, k, v, kc, vc, bias, layer, first, cls_of):
    b, s, w = q.shape
    l = kc.shape[1]
    npair = w // LANES
    tq = NA_QROWS * GRID_W
    nblk = s // tq
    ncls = bias.shape[2]
    sample = lambda n: pl.BlockSpec((1, n, LANES), lambda p, i, *_: (i, 0, p))
    grid_spec = pltpu.PrefetchScalarGridSpec(
        num_scalar_prefetch=2, grid=(npair, b),
        in_specs=[sample(s), sample(s), sample(s), sample(l), sample(l),
                  pl.BlockSpec((None, None, ncls, 2 * tq, NA_WIN_ROWS * GRID_W),
                               lambda p, i, *_: (layer, p, 0, 0, 0))],
        out_specs=sample(s))
    return pl.pallas_call(
        functools.partial(_na_kernel, nblk=nblk), grid_spec=grid_spec,
        out_shape=jax.ShapeDtypeStruct((b, s, w), BF),
        compiler_params=_cparams(2, 48),
        name="nbr_attn",
    )(jnp.asarray(first), jnp.asarray(cls_of), q, k, v, kc, vc, bias)


def _stack_heads(q4):
    first = _half_mask(LANES)
    parts = []
    for g in range(SW_GROUP):
        pair = q4[:, (g // 2) * LANES:(g // 2 + 1) * LANES]
        sel = first if g % 2 == 0 else jnp.logical_not(first)
        parts.append(jnp.where(sel, pair, jnp.zeros_like(pair)))
    return jnp.concatenate(parts, axis=0)


def _unstack_heads(o, n):
    first = _half_mask(LANES)
    pairs = [jnp.where(first, o[(2 * j) * n:(2 * j + 1) * n], o[(2 * j + 1) * n:(2 * j + 2) * n])
             for j in range(SW_GROUP // 2)]
    return jnp.concatenate(pairs, axis=1)


def _sink_column(sink_ref, base, n):
    return jnp.concatenate([jnp.full((n, 1), sink_ref[base + g], F32) for g in range(SW_GROUP)], axis=0)


def _sw_kernel(sink_ref, q_ref, k_ref, v_ref, kc_ref, vc_ref, o_ref, *, nb, sink_base):
    tq = SW_BLOCK
    gw = SW_GROUP * HEAD_DIM
    row = lax.broadcasted_iota(jnp.int32, (SW_GROUP * tq, tq), 0) & (tq - 1)
    col = lax.broadcasted_iota(jnp.int32, (SW_GROUP * tq, tq), 1)
    band = {-1: col >= row, 1: col <= row}
    sinks = [_sink_column(sink_ref, sink_base + h * SW_GROUP, tq) for h in range(SW_KV_HEADS)]

    def body(n, carry):
        q0 = pl.multiple_of(n * tq, tq)
        outs = []
        for h in range(SW_KV_HEADS):
            lanes = slice(h * LANES, (h + 1) * LANES)
            qs = _stack_heads(q_ref[0, pl.ds(q0, tq), h * gw:(h + 1) * gw])
            scores, values = [], []
            for c in (-1, 0, 1):
                off = pl.multiple_of(jnp.clip(n + c, 0, nb - 1) * tq, tq)
                s = _nt(qs, k_ref[0, pl.ds(off, tq), lanes])
                if c != 0:
                    inside = jnp.logical_and(n + c >= 0, n + c <= nb - 1)
                    s = jnp.where(jnp.logical_and(band[c], inside), s, NEG)
                scores.append(s)
                values.append(v_ref[0, pl.ds(off, tq), lanes])
            scores.append(_nt(qs, kc_ref[0, :, lanes]))
            values.append(vc_ref[0, :, lanes])
            outs.append(_unstack_heads(_softmax_pv(scores, values, sinks[h]), tq))
        o_ref[0, pl.ds(q0, tq), :] = jnp.concatenate(outs, axis=1).astype(BF)
        return carry

    lax.fori_loop(0, nb, body, 0)


def _sw_call(q, k, v, kc, vc, sink_flat, layer):
    b, s, w = q.shape
    l = kc.shape[1]
    blk = lambda n, width: pl.BlockSpec((1, n, width), lambda i: (i, 0, 0))
    kvw = SW_KV_HEADS * LANES
    return pl.pallas_call(
        functools.partial(_sw_kernel, nb=s // SW_BLOCK, sink_base=layer * SW_Q_HEADS),
        grid=(b,),
        in_specs=[pl.BlockSpec(memory_space=pltpu.SMEM), blk(s, w), blk(s, kvw), blk(s, kvw),
                  blk(l, kvw), blk(l, kvw)],
        out_specs=blk(s, w),
        out_shape=jax.ShapeDtypeStruct((b, s, w), BF),
        compiler_params=_cparams(1, 40),
        name="window_attn",
    )(sink_flat, q, k, v, kc, vc)


def _ctx_attn_kernel(sink_ref, naq_ref, nak_ref, nav_ref, swq_ref, swk_ref, swv_ref, ob_ref, oc_ref, *, sink_base):
    n = naq_ref.shape[1]
    for p in range(NA_HEADS // 2):
        lanes = slice(p * LANES, (p + 1) * LANES)
        s = _nt(_stack_pair(naq_ref[0, :, lanes]), nak_ref[0, :, lanes])
        ob_ref[0, :, lanes] = _unstack_pair(_softmax_pv([s], [nav_ref[0, :, lanes]]), n).astype(BF)
    gw = SW_GROUP * HEAD_DIM
    for h in range(SW_KV_HEADS):
        lanes = slice(h * LANES, (h + 1) * LANES)
        s = _nt(_stack_heads(swq_ref[0, :, h * gw:(h + 1) * gw]), swk_ref[0, :, lanes])
        sink = _sink_column(sink_ref, sink_base + h * SW_GROUP, n)
        o = _softmax_pv([s], [swv_ref[0, :, lanes]], sink)
        oc_ref[0, :, h * gw:(h + 1) * gw] = _unstack_heads(o, n).astype(BF)


def _ctx_attn_call(naq, nak, nav, swq, swk, swv, sink_flat, layer):
    b, l, w = naq.shape
    blk = lambda width: pl.BlockSpec((1, l, width), lambda i: (i, 0, 0))
    return pl.pallas_call(
        functools.partial(_ctx_attn_kernel, sink_base=layer * SW_Q_HEADS), grid=(b,),
        in_specs=[pl.BlockSpec(memory_space=pltpu.SMEM), blk(w), blk(w), blk(w),
                  blk(w), blk(2 * LANES), blk(2 * LANES)],
        out_specs=[blk(w), blk(w)],
        out_shape=[jax.ShapeDtypeStruct((b, l, w), BF)] * 2,
        compiler_params=_cparams(1, 32),
        name="ctx_attn",
    )(sink_flat, naq, nak, nav, swq, swk, swv)


def _merge_kernel(oa_ref, ob_ref, oc_ref, gate_ref, x_ref, mod_ref, g_ref, wb_ref, wo_ref, wr_ref,
                  x1_ref, h2_ref, aff_ref):
    d = x_ref.shape[2]
    merged = None
    for j, o_ref in enumerate((oa_ref, ob_ref, oc_ref)):
        term = gate_ref[0, :, j * d:(j + 1) * d].astype(F32) * _mm(o_ref[0], wb_ref[j])
        merged = term if merged is None else merged + term
    m = mod_ref[0]
    x1 = x_ref[0] + m[2:3, :] * _mm(merged.astype(BF), wo_ref[...])
    x1_ref[0] = x1
    ms = jnp.mean(x1 * x1, axis=-1, keepdims=True)
    h2 = ((x1 * lax.rsqrt(ms + RMS_EPS)) * g_ref[...]) * (1.0 + m[4:5, :]) + m[3:4, :]
    h2b = h2.astype(BF)
    h2_ref[0] = h2b
    logits = _nt(wr_ref[...], h2b)
    e = jnp.exp(logits - jnp.max(logits, axis=0, keepdims=True))
    aff_ref[0] = e / jnp.sum(e, axis=0, keepdims=True)


def _merge_call(oa, ob, oc, gates, x, mod, layer, mod_row, g, wb, wo, wr_t, *, tm):
    b, n, d = x.shape
    tok = lambda w: pl.BlockSpec((1, tm, w), lambda i, t: (i, t, 0))
    return pl.pallas_call(
        _merge_kernel, grid=(b, n // tm),
        in_specs=[tok(BRANCH), tok(BRANCH), tok(BRANCH), tok(3 * d), tok(d), _mod_spec(layer, mod_row, d),
                  _layer_spec(g, layer), _layer_spec(wb, layer), _layer_spec(wo, layer), _layer_spec(wr_t, layer)],
        out_specs=[tok(d), tok(d), pl.BlockSpec((1, N_EXPERTS, tm), lambda i, t: (i, 0, t))],
        out_shape=[jax.ShapeDtypeStruct((b, n, d), F32), jax.ShapeDtypeStruct((b, n, d), BF),
                   jax.ShapeDtypeStruct((b, N_EXPERTS, n), F32)],
        compiler_params=_cparams(2, 48),
        name="merge_norm_router",
    )(oa, ob, oc, gates, x, mod, g, wb, wo, wr_t)


def _route_consts(nch, seg_rows, cap, gpb):
    r = N_EXPERTS * nch
    idx = np.arange(r)
    seg = idx // seg_rows
    same = seg[:, None] == seg[None, :]
    before = same & (idx[None, :] < idx[:, None])
    lane = np.arange(gpb * LANES)
    same_grp = (lane[:, None] // LANES) == (lane[None, :] // LANES)
    upper = same_grp & (lane[:, None] < lane[None, :])
    base = ((idx % nch) // seg_rows * cap).astype(np.float32).reshape(r, 1)
    return (jnp.asarray(same, BF), jnp.asarray(before, BF), jnp.asarray(upper, BF), jnp.asarray(same_grp, BF),
            jnp.asarray(base))


def _route_kernel(aff_ref, same_ref, before_ref, upper_ref, ones_ref, base_ref, pos_ref, *, cap, gpb):
    bits = jnp.concatenate([lax.bitcast_convert_type(aff_ref[g], jnp.int32) for g in range(gpb)], axis=1)
    same, before, upper, ones = same_ref[...], before_ref[...], upper_ref[...], ones_ref[...]

    def as_mask(cond):
        return jnp.where(cond, 1.0, 0.0).astype(BF)

    def lane_total(col_counts):
        return _mm(col_counts.astype(BF), ones)

    def seg_total(mask_bf):
        return lane_total(_mm(same, mask_bf))

    def prefix(mask_bf):
        return _mm(mask_bf, upper) + lane_total(_mm(before, mask_bf))

    def body(i, thr):
        cand = thr | lax.shift_left(jnp.int32(1), 30 - i)
        return jnp.where(seg_total(as_mask(bits >= cand)) >= cap, cand, thr)

    thr = lax.fori_loop(0, 31, body, jnp.zeros(bits.shape, jnp.int32))
    gt = bits > thr
    eq = bits == thr
    need = cap - seg_total(as_mask(gt))
    sel = jnp.logical_or(gt, jnp.logical_and(eq, prefix(as_mask(eq)) < need))
    slot = jnp.where(sel, prefix(as_mask(sel)) + base_ref[...], -1.0)
    for g in range(gpb):
        pos_ref[g] = slot[:, g * LANES:(g + 1) * LANES]


def _route_call(aff_rows, nch, seg_rows, cap):
    g, r, _ = aff_rows.shape
    gpb = max(n for n in (1, 2, 4) if g % n == 0)
    consts = _route_consts(nch, seg_rows, cap, gpb)
    full = lambda a: pl.BlockSpec(a.shape, lambda i: (0,) * a.ndim)
    return pl.pallas_call(
        functools.partial(_route_kernel, cap=cap, gpb=gpb), grid=(g // gpb,),
        in_specs=[pl.BlockSpec((gpb, r, LANES), lambda i: (i, 0, 0))] + [full(a) for a in consts],
        out_specs=pl.BlockSpec((gpb, r, LANES), lambda i: (i, 0, 0)),
        out_shape=jax.ShapeDtypeStruct((g, r, LANES), F32),
        compiler_params=_cparams(1, 32),
        name="expert_choice_select",
    )(aff_rows, *consts)


def _moe_kernel(h_ref, pos_ref, post_ref, aff_ref, wg_ref, wu_ref, wd_ref, y_ref, *, nslots, rows_per_store):
    e = pl.program_id(1)

    @pl.when(e == 0)
    def _():
        y_ref[...] = jnp.zeros_like(y_ref)

    pos = pos_ref[0, 0]
    aff = aff_ref[0, 0]
    nch = pos.shape[0]
    slot_col = lax.broadcasted_iota(jnp.int32, (nslots, LANES), 0).astype(F32)
    onehot, gate = [], jnp.zeros((nslots, 1), F32)
    for c in range(nch):
        hit = slot_col == pos[c:c + 1, :]
        onehot.append(jnp.where(hit, 1.0, 0.0).astype(BF))
        gate = gate + jnp.sum(jnp.where(hit, aff[c:c + 1, :], 0.0), axis=1, keepdims=True)
    xe = _mm(jnp.concatenate(onehot, axis=1), h_ref[0]).astype(BF)
    a = _mm(xe, wg_ref[0])
    u = _mm(xe, wu_ref[0])
    mid = ((a * jax.nn.sigmoid(a)) * u).astype(BF)
    ye = (_mm(mid, wd_ref[0]) * gate).astype(BF)

    post = post_ref[0, 0]
    slot_row = lax.broadcasted_iota(jnp.int32, (LANES, nslots), 1).astype(F32)
    per = rows_per_store // LANES
    for c0 in range(0, nch, per):
        scat = jnp.concatenate(
            [jnp.where(post[:, c:c + 1] == slot_row, 1.0, 0.0).astype(BF) for c in range(c0, min(c0 + per, nch))],
            axis=0)
        lo = c0 * LANES
        y_ref[0, lo:lo + scat.shape[0], :] += _mm(scat, ye)


def _moe_call(h2, pos, post, aff, wg, wu, wd, layer, nslots):
    g, n, d = h2.shape
    nch = n // LANES
    ff = wg.shape[3]
    tile = lambda: pl.BlockSpec((1, 1, nch, LANES), lambda i, e: (i, e, 0, 0))
    return pl.pallas_call(
        functools.partial(_moe_kernel, nslots=nslots, rows_per_store=min(n, 512)),
        grid=(g, N_EXPERTS),
        in_specs=[pl.BlockSpec((1, n, d), lambda i, e: (i, 0, 0)),
                  tile(), pl.BlockSpec((1, 1, LANES, nch), lambda i, e: (i, e, 0, 0)), tile(),
                  pl.BlockSpec((None, 1, d, ff), lambda i, e: (layer, e, 0, 0)),
                  pl.BlockSpec((None, 1, d, ff), lambda i, e: (layer, e, 0, 0)),
                  pl.BlockSpec((None, 1, ff, d), lambda i, e: (layer, e, 0, 0))],
        out_specs=pl.BlockSpec((1, n, d), lambda i, e: (i, 0, 0)),
        out_shape=jax.ShapeDtypeStruct((g, n, d), F32),
        compiler_params=_cparams(2, 56),
        name="expert_ffn",
    )(h2, pos, post, aff, wg, wu, wd)


def _expert_choice(h2, aff, wg, wu, wd, layer, group):
    b, n, d = h2.shape
    cap = CAPACITY_FACTOR * n // N_EXPERTS
    g = b // group
    ng = group * n
    nch = ng // LANES
    aff_g = aff.reshape(g, group, N_EXPERTS, n).transpose(0, 2, 1, 3).reshape(g, N_EXPERTS * nch, LANES)
    pos = _route_call(aff_g, nch, n // LANES, cap)
    pos4 = pos.reshape(g, N_EXPERTS, nch, LANES)
    y = _moe_call(h2.reshape(g, ng, d), pos4, pos4.transpose(0, 1, 3, 2),
                  aff_g.reshape(g, N_EXPERTS, nch, LANES), wg, wu, wd, layer, group * cap)
    return y.reshape(b, n, d)


def _final_kernel(x_ref, y_ref, mod_ref, g_ref, o_ref):
    x = x_ref[0] + mod_ref[0][5:6, :] * y_ref[0]
    ms = jnp.mean(x * x, axis=-1, keepdims=True)
    o_ref[0] = (x * lax.rsqrt(ms + RMS_EPS)) * g_ref[...]


def _final_call(x, y, mod, layer, g, *, tm):
    b, n, d = x.shape
    tok = pl.BlockSpec((1, tm, d), lambda i, t: (i, t, 0))
    return pl.pallas_call(
        _final_kernel, grid=(b, n // tm),
        in_specs=[tok, tok, _mod_spec(layer, None, d), pl.BlockSpec((1, d), lambda i, t: (0, 0))],
        out_specs=tok, out_shape=jax.ShapeDtypeStruct((b, n, d), F32),
        compiler_params=_cparams(2, 32),
        name="final_norm",
    )(x, y, mod, g.reshape(1, d))


def _rope_tables(s):
    t = np.arange(s)
    row = (t // GRID_W).astype(np.float32)
    col = (t % GRID_W).astype(np.float32)
    nfreq = HEAD_DIM // 4
    inv = (ROPE_BASE ** (-jnp.arange(nfreq, dtype=F32) / nfreq))
    ang = jnp.concatenate([jnp.asarray(row)[:, None] * inv, jnp.asarray(col)[:, None] * inv], axis=-1)
    cos, sin = jnp.cos(ang), jnp.sin(ang)
    cosw = jnp.concatenate([cos, cos] * 2, axis=-1)
    sinw = jnp.concatenate([-sin, sin] * 2, axis=-1)
    return cosw, sinw


def _extend_w_in(w_in):
    k = w_in[..., C_SWK:C_SWK + SW_KV_HEADS * HEAD_DIM]
    v = w_in[..., C_SWK + SW_KV_HEADS * HEAD_DIM:C_SWK + 2 * SW_KV_HEADS * HEAD_DIM]
    dup = lambda a: jnp.concatenate([a[..., :HEAD_DIM]] * 2 + [a[..., HEAD_DIM:]] * 2, axis=-1)
    rest = w_in[..., C_SWK + 2 * SW_KV_HEADS * HEAD_DIM:]
    return jnp.concatenate([w_in[..., :C_SWK], dup(k), dup(v), rest], axis=-1).astype(BF)


def _ctx_group(b, l):
    g = max(1, min(b, 2048 // l))
    while b % g:
        g -= 1
    return g


def kernel(x, c, ctx, c_ctx, w_ada, b_ada, g_mix, w_in, conv_w, na_rpb, sw_sink, w_branch, w_out,
           g_ffn, w_router, w_exp_gate, w_exp_up, w_exp_down, g_final):
    bsz, s, d = x.shape
    l = ctx.shape[1]
    depth = w_ada.shape[0]
    assert d == 1024 and s % (NA_QROWS * GRID_W) == 0 and s // GRID_W >= NA_WIN_ROWS and l % LANES == 0
    tm = 512 if s % 512 == 0 else 256
    tmc = min(256, l)

    mod_rows = -(-(bsz + 1) // 8) * 8
    cs = jnp.zeros((mod_rows, d), F32).at[:bsz].set(c).at[bsz].set(c_ctx)
    mod = _ada_call(cs, w_ada.astype(BF), b_ada).reshape(depth, mod_rows, 6, d)

    w_ext = _extend_w_in(w_in)
    wb, wo = w_branch.astype(BF), w_out.astype(BF)
    wr_t = jnp.swapaxes(w_router, 1, 2).astype(BF)
    wg, wu, wd = w_exp_gate.astype(BF), w_exp_up.astype(BF), w_exp_down.astype(BF)
    g_mix3, g_ffn3 = g_mix.reshape(depth, 1, d), g_ffn.reshape(depth, 1, d)
    sink_flat = sw_sink.reshape(-1)
    cosw, sinw = _rope_tables(s)
    first, cls_of, cls_tiles = _na_layout(s // GRID_W)
    bias = _na_bias(na_rpb, cls_tiles)
    cgroup = _ctx_group(bsz, l)

    xc = ctx
    y = yc = None
    for li in range(depth):
        last = li == depth - 1
        x, (ab, u, naq, nak, nav, swq, swk, swv, gates) = _inproj_call(
            x, y, mod, li, None, g_mix3, w_ext, cosw, sinw, tm=tm)
        xc, (cab, cu, cnaq, cnak, cnav, cswq, cswk, cswv, cgates) = _inproj_call(
            xc, yc, mod, li, bsz, g_mix3, w_ext, None, None, tm=tmc)

        o_a = _conv_call(ab, u, conv_w, li)
        o_b = _na_call(naq, nak, nav, cnak, cnav, bias, li, first, cls_of)
        o_c = _sw_call(swq, swk, swv, cswk, cswv, sink_flat, li)
        x, h2, aff = _merge_call(o_a, o_b, o_c, gates, x, mod, li, None, g_ffn3, wb, wo, wr_t, tm=tm)
        y = _expert_choice(h2, aff, wg, wu, wd, li, 1)
        if not last:
            co_a = _conv_call(cab, cu, conv_w, li)
            co_b, co_c = _ctx_attn_call(cnaq, cnak, cnav, cswq, cswk, cswv, sink_flat, li)
            xc, h2c, affc = _merge_call(co_a, co_b, co_c, cgates, xc, mod, li, bsz, g_ffn3, wb, wo, wr_t, tm=tmc)
            yc = _expert_choice(h2c, affc, wg, wu, wd, li, cgroup)
    return _final_call(x, y, mod, depth - 1, g_final, tm=tm)
```

```python
import functools

import numpy as np
import jax
import jax.numpy as jnp
from jax import lax
from jax.experimental import pallas as pl
from jax.experimental.pallas import tpu as pltpu

BF = jnp.bfloat16
F32 = jnp.float32

GRID_W = 64
HEAD_DIM = 64
BRANCH = 512
NA_HEADS = 8
NA_ROWS = 8
NA_COLS = 16
NA_QROWS = 4
NA_WIN_ROWS = NA_QROWS + NA_ROWS
SW_Q_HEADS = 8
SW_KV_HEADS = 2
SW_GROUP = SW_Q_HEADS // SW_KV_HEADS
SW_BLOCK = 128
N_EXPERTS = 16
CAPACITY_FACTOR = 2
ROPE_BASE = 10000.0
RMS_EPS = 1e-6
NEG = -1e30
QK_SCALE = HEAD_DIM ** -0.5

LANES = 128
V7X_VMEM_BYTES = 64 * 1024 * 1024

C_AB, C_AC, C_AX = 0, 512, 1024
C_NAQ, C_NAK, C_NAV = 1536, 2048, 2560
C_SWQ, C_SWK, C_SWV = 3072, 3584, 3840
C_GATE, C_END = 4096, 7168


def _cparams(n_axes, vmem_mb):
    return pltpu.CompilerParams(dimension_semantics=("arbitrary",) * n_axes,
                                vmem_limit_bytes=vmem_mb * 1024 * 1024)


def _nt(a, b):
    return lax.dot_general(a, b, (((1,), (1,)), ((), ())), preferred_element_type=F32)


def _mm(a, b):
    return jnp.dot(a, b, preferred_element_type=F32)


def _half_mask(width):
    lane = lax.broadcasted_iota(jnp.int32, (1, width), 1)
    return (lane & (LANES - 1)) < HEAD_DIM


def _lane_blocks(s):
    return [s[:, j:j + LANES] for j in range(0, s.shape[1], LANES)]


def _softmax_pv(scores, values, sink=None):
    return _weighted_values(*_softmax_parts(scores, sink), values, sink)


def _softmax_parts(scores, sink=None, sum_on_vpu=False):
    blocks = [blk for s in scores for blk in _lane_blocks(s)]
    mv = blocks[0]
    for blk in blocks[1:]:
        mv = jnp.maximum(mv, blk)
    shift = jnp.broadcast_to(jnp.max(mv, axis=-1, keepdims=True), mv.shape)
    if sink is not None:
        shift = jnp.maximum(shift, sink)
    probs, folded = [], None
    for s in scores:
        pblocks = []
        for blk in _lane_blocks(s):
            p = jnp.exp(blk - shift)
            if sum_on_vpu:
                folded = p if folded is None else folded + p
            pblocks.append(p.astype(BF))
        probs.append(pblocks[0] if len(pblocks) == 1 else jnp.concatenate(pblocks, axis=1))
    return probs, shift, folded


def _weighted_values(probs, shift, folded, values, sink=None):
    acc = None
    for p, v in zip(probs, values):
        t = _mm(p, v if folded is not None else jnp.concatenate([v, jnp.ones_like(v)], axis=1))
        acc = t if acc is None else acc + t
    total = jnp.sum(folded, axis=-1, keepdims=True) if folded is not None else acc[:, LANES:]
    if sink is not None:
        total = total + jnp.exp(sink - shift)
    return acc[:, :LANES] * (1.0 / total)


def _ada_kernel(cs_ref, w_ref, b_ref, o_ref):
    c = cs_ref[...]
    s = (c * jax.nn.sigmoid(c)).astype(BF)
    o_ref[0] = _mm(s, w_ref[0]) + b_ref[0]


def _ada_call(cs, w_ada_bf, b_ada):
    depth, d, n6 = w_ada_bf.shape
    rows = cs.shape[0]
    tn = 1536
    return pl.pallas_call(
        _ada_kernel,
        grid=(depth, n6 // tn),
        in_specs=[pl.BlockSpec((rows, d), lambda l, j: (0, 0)),
                  pl.BlockSpec((1, d, tn), lambda l, j: (l, 0, j)),
                  pl.BlockSpec((1, 1, tn), lambda l, j: (l, 0, j))],
        out_specs=pl.BlockSpec((1, rows, tn), lambda l, j: (l, 0, j)),
        out_shape=jax.ShapeDtypeStruct((depth, rows, n6), F32),
        compiler_params=_cparams(2, 32),
        name="adaln_mod",
    )(cs, w_ada_bf, b_ada.reshape(depth, 1, n6))


def _rope(z, cos, sin_signed):
    w = z.shape[1]
    rep = w // LANES
    cosw = jnp.concatenate([cos] * rep, axis=1) if rep > 1 else cos
    sinw = jnp.concatenate([sin_signed] * rep, axis=1) if rep > 1 else sin_signed
    lane = lax.broadcasted_iota(jnp.int32, (1, w), 1)
    first = (lane & (HEAD_DIM - 1)) < HEAD_DIM // 2
    partner = jnp.where(first, pltpu.roll(z, w - HEAD_DIM // 2, 1), pltpu.roll(z, HEAD_DIM // 2, 1))
    return z * cosw + partner * sinw


def _inproj_kernel(*refs, has_y, latent):
    it = iter(refs)
    x_ref = next(it)
    if has_y:
        y_ref, modp_ref = next(it), next(it)
    mod_ref, g_ref, w_ref = next(it), next(it), next(it)
    if latent:
        cos_ref, sin_ref = next(it), next(it)
    if has_y:
        xnew_ref = next(it)
    ab_ref, u_ref, naq_ref, nak_ref, nav_ref, swq_ref, swk_ref, swv_ref, gate_ref = (next(it) for _ in range(9))
    if latent:
        nak8_ref, nav8_ref = next(it), next(it)

    x = x_ref[0]
    if has_y:
        x = x + modp_ref[0][5:6, :] * y_ref[0]
        xnew_ref[0] = x
    m = mod_ref[0]
    ms = jnp.mean(x * x, axis=-1, keepdims=True)
    h = ((x * lax.rsqrt(ms + RMS_EPS)) * g_ref[...]) * (1.0 + m[1:2, :]) + m[0:1, :]
    hb = h.astype(BF)

    def proj(lo, hi):
        return _mm(hb, w_ref[:, lo:hi])

    ab_ref[0] = proj(C_AB, C_AC)
    u_ref[0] = proj(C_AC, C_AX) * proj(C_AX, C_NAQ)
    naq_ref[0] = (proj(C_NAQ, C_NAK) * QK_SCALE).astype(BF)
    nk = proj(C_NAK, C_NAV)
    nv = proj(C_NAV, C_SWQ)
    nak_ref[0] = nk.astype(BF)
    nav_ref[0] = nv.astype(BF)
    if latent:
        nak8_ref[0] = jnp.concatenate([nk[NA_SHIFT:], nk[:NA_SHIFT]], axis=0).astype(BF)
        nav8_ref[0] = jnp.concatenate([nv[NA_SHIFT:], nv[:NA_SHIFT]], axis=0).astype(BF)
    q = proj(C_SWQ, C_SWK)
    k = proj(C_SWK, C_SWV)
    if latent:
        cos, sin = cos_ref[...], sin_ref[...]
        q = _rope(q, cos, sin)
        k = _rope(k, cos, sin)
    swq_ref[0] = (q * QK_SCALE).astype(BF)
    swk_ref[0] = k.astype(BF)
    swv_ref[0] = proj(C_SWV, C_GATE).astype(BF)
    gate_ref[0] = jax.nn.sigmoid(proj(C_GATE, C_END)).astype(BF)


def _mod_spec(layer, mod_row, d):
    if mod_row is None:
        return pl.BlockSpec((None, 1, 6, d), lambda i, t: (layer, i, 0, 0))
    return pl.BlockSpec((None, 1, 6, d), lambda i, t: (layer, mod_row, 0, 0))


def _layer_spec(arr, layer):
    zeros = (0,) * (arr.ndim - 1)
    return pl.BlockSpec((None,) + arr.shape[1:], lambda i, t: (layer,) + zeros, pipeline_mode=pl.Buffered(1))


def _inproj_call(x, y, mod, layer, mod_row, g, w_ext, cos, sin, *, tm):
    b, n, d = x.shape
    has_y = y is not None
    latent = cos is not None
    assert not latent or tm % GRID_W == 0
    nt = n // tm
    tok = lambda w: pl.BlockSpec((1, tm, w), lambda i, t: (i, t, 0))
    args, in_specs = [x], [tok(d)]
    if has_y:
        args += [y, mod]
        in_specs += [tok(d), _mod_spec(layer - 1, mod_row, d)]
    args += [mod, g, w_ext]
    in_specs += [_mod_spec(layer, mod_row, d), _layer_spec(g, layer), _layer_spec(w_ext, layer)]
    if latent:
        args += [cos, sin]
        in_specs += [pl.BlockSpec((tm, LANES), lambda i, t: (t, 0))] * 2
    widths = [(BRANCH, F32), (BRANCH, F32), (BRANCH, BF), (BRANCH, BF), (BRANCH, BF),
              (BRANCH, BF), (2 * LANES, BF), (2 * LANES, BF), (3 * d, BF)]
    if latent:
        widths += [(BRANCH, BF), (BRANCH, BF)]
    out_shape = [jax.ShapeDtypeStruct((b, n, w), dt) for w, dt in widths]
    out_specs = [tok(w) for w, _ in widths]
    if has_y:
        out_shape = [jax.ShapeDtypeStruct((b, n, d), F32)] + out_shape
        out_specs = [tok(d)] + out_specs
    outs = pl.pallas_call(
        functools.partial(_inproj_kernel, has_y=has_y, latent=latent),
        grid=(b, nt), in_specs=in_specs, out_specs=out_specs, out_shape=out_shape,
        compiler_params=_cparams(2, 56),
        name="inproj",
    )(*args)
    if has_y:
        return outs[0], outs[1:]
    return x, outs


def _conv_kernel(ab_ref, u_ref, w_ref, o_ref):
    u = u_ref[0]
    n = u.shape[0]
    t = lax.broadcasted_iota(jnp.int32, (n, 1), 0)
    prev = jnp.where(t == 0, 0.0, pltpu.roll(u, 1, 0))
    nxt = jnp.where(t == n - 1, 0.0, pltpu.roll(u, n - 1, 0))
    w = w_ref[...]
    o_ref[0] = (ab_ref[0] * (w[0:1, :] * prev + w[1:2, :] * u + w[2:3, :] * nxt)).astype(BF)


def _conv_call(ab, u, w, layer):
    b, n, c = ab.shape
    blk = pl.BlockSpec((1, n, c), lambda i: (i, 0, 0))
    return pl.pallas_call(
        _conv_kernel, grid=(b,),
        in_specs=[blk, blk, pl.BlockSpec((None, 3, c), lambda i: (layer, 0, 0))],
        out_specs=blk, out_shape=jax.ShapeDtypeStruct((b, n, c), BF),
        compiler_params=_cparams(1, 48),
        name="short_conv",
    )(ab, u, w)


def _na_layout(rows):
    nblk = rows // NA_QROWS
    first, tiles = [], []
    for i in range(nblk):
        ks = int(np.clip(NA_QROWS * i - NA_ROWS // 2, 0, rows - NA_WIN_ROWS))
        tile = np.full((NA_QROWS, NA_WIN_ROWS), 2 * NA_ROWS - 1, np.int32)
        for rq in range(NA_QROWS):
            r = NA_QROWS * i + rq
            rs = int(np.clip(r - NA_ROWS // 2, 0, rows - NA_ROWS))
            for rk in range(NA_WIN_ROWS):
                ra = ks + rk
                if rs <= ra < rs + NA_ROWS:
                    tile[rq, rk] = ra - r + NA_ROWS - 1
        first.append(ks)
        tiles.append(tile)
    classes, cls_of = [], []
    for tle in tiles:
        for ci, c in enumerate(classes):
            if np.array_equal(c, tle):
                cls_of.append(ci)
                break
        else:
            classes.append(tle)
            cls_of.append(len(classes) - 1)
    return np.array(first, np.int32), np.array(cls_of, np.int32), np.stack(classes)


NA_QCOLS = 16
NA_KCOLS = 32
NA_SHIFT = 8
NA_BLOCKS_PER_ITER = 2
NA_KSTART =tuple(int(np.clip(NA_QCOLS * j - NA_COLS // 2, 0, GRID_W - NA_KCOLS)) for j in range(GRID_W // NA_QCOLS))


def _na_bias(na_rpb, cls_tiles):
    depth, nh = na_rpb.shape[:2]
    qc = np.arange(GRID_W)[:, None]
    kc = np.arange(GRID_W)[None, :]
    toe = np.clip(kc - qc + NA_COLS - 1, 0, 2 * NA_COLS - 2)
    cstart = np.clip(qc - NA_COLS // 2, 0, GRID_W - NA_COLS)
    col_ok = (kc >= cstart) & (kc < cstart + NA_COLS)
    m = jnp.take(na_rpb, jnp.asarray(toe.reshape(-1)), axis=-1)
    m = m.reshape(depth, nh, 2 * NA_ROWS - 1, GRID_W, GRID_W)
    m = jnp.where(jnp.asarray(col_ok), m, NEG)
    m = jnp.concatenate([m, jnp.full((depth, nh, 1, GRID_W, GRID_W), NEG, F32)], axis=2)
    ncls = cls_tiles.shape[0]
    per_block = []
    for j, ks in enumerate(NA_KSTART):
        mj = m[:, :, :, NA_QCOLS * j:NA_QCOLS * (j + 1), ks:ks + NA_KCOLS]
        b = jnp.take(mj, jnp.asarray(cls_tiles.reshape(-1)), axis=2)
        b = b.reshape(depth, nh, ncls, NA_QROWS, NA_WIN_ROWS, NA_QCOLS, NA_KCOLS)
        b = b.transpose(0, 1, 2, 3, 5, 4, 6)
        b = b.reshape(depth, nh // 2, 2, ncls, NA_QROWS * NA_QCOLS, NA_WIN_ROWS * NA_KCOLS)
        per_block.append(b.transpose(0, 1, 3, 2, 4, 5).reshape(
            depth, nh // 2, ncls, 2 * NA_QROWS * NA_QCOLS, NA_WIN_ROWS * NA_KCOLS))
    return jnp.stack(per_block, axis=3)


def _stack_pair(q):
    first = _half_mask(LANES)
    zero = jnp.zeros_like(q)
    return jnp.concatenate([jnp.where(first, q, zero), jnp.where(first, zero, q)], axis=0)


def _unstack_pair(o, n):
    return jnp.where(_half_mask(LANES), o[:n], o[n:])


def _na_kernel(first_ref, cls_ref, q_ref, k_ref, v_ref, k8_ref, v8_ref, kc_ref, vc_ref, bias_ref, o_ref, *, nblk):
    tq = NA_QROWS * GRID_W
    kc = kc_ref[0]
    vc = vc_ref[0]

    def rows(ref, base, step, width, count):
        return jnp.concatenate(
            [ref[0, pl.ds(pl.multiple_of(base + r * GRID_W + step, 2 * NA_SHIFT), width), :] for r in range(count)],
            axis=0)

    def body(it, carry):
        scores, values, dests = [], [], []
        for sub in range(NA_BLOCKS_PER_ITER):
            i = it * NA_BLOCKS_PER_ITER + sub
            k0 = pl.multiple_of(first_ref[i] * GRID_W, tq)
            q0 = pl.multiple_of(i * tq, tq)
            cls = cls_ref[i]
            for j, ks in enumerate(NA_KSTART):
                aligned = ks % (2 * NA_SHIFT) == 0
                ksrc, vsrc = (k_ref, v_ref) if aligned else (k8_ref, v8_ref)
                koff = ks if aligned else ks - NA_SHIFT
                qs = _stack_pair(rows(q_ref, q0, NA_QCOLS * j, NA_QCOLS, NA_QROWS))
                s_loc = _nt(qs, rows(ksrc, k0, koff, NA_KCOLS, NA_WIN_ROWS)) + bias_ref[cls, j]
                scores.append([s_loc, _nt(qs, kc)])
                values.append([rows(vsrc, k0, koff, NA_KCOLS, NA_WIN_ROWS), vc])
                dests.append(q0 + NA_QCOLS * j)
        parts = [_softmax_parts(s, sum_on_vpu=True) for s in scores]
        for part, v, dst in zip(parts, values, dests):
            o = _unstack_pair(_weighted_values(*part, v), NA_QROWS * NA_QCOLS).astype(BF)
            for r in range(NA_QROWS):
                o_ref[0, pl.ds(pl.multiple_of(dst + r * GRID_W, NA_QCOLS), NA_QCOLS), :] = (
                    o[r * NA_QCOLS:(r + 1) * NA_QCOLS])
        return carry

    lax.fori_loop(0, nblk // NA_BLOCKS_PER_ITER, body, 0)


def _na_call(q, k, v, k8, v8, kc, vc, bias, layer, first, cls_of):
    b, s, w = q.shape
    l = kc.shape[1]
    npair = w // LANES
    nblk = s // (NA_QROWS * GRID_W)
    sample = lambda n: pl.BlockSpec((1, n, LANES), lambda p, i, *_: (i, 0, p))
    grid_spec = pltpu.PrefetchScalarGridSpec(
        num_scalar_prefetch=2, grid=(npair, b),
        in_specs=[sample(s)] * 5 + [sample(l), sample(l),
                                    pl.BlockSpec((None, None) + bias.shape[2:],
                                                 lambda p, i, *_: (layer, p, 0, 0, 0, 0))],
        out_specs=sample(s))
    return pl.pallas_call(
        functools.partial(_na_kernel, nblk=nblk), grid_spec=grid_spec,
        out_shape=jax.ShapeDtypeStruct((b, s, w), BF),
        compiler_params=_cparams(2, 48),
        name="nbr_attn",
    )(jnp.asarray(first), jnp.asarray(cls_of), q, k, v, k8, v8, kc, vc, bias)


def _stack_heads(q4):
    first = _half_mask(LANES)
    parts = []
    for g in range(SW_GROUP):
        pair = q4[:, (g // 2) * LANES:(g // 2 + 1) * LANES]
        sel = first if g % 2 == 0 else jnp.logical_not(first)
        parts.append(jnp.where(sel, pair, jnp.zeros_like(pair)))
    return jnp.concatenate(parts, axis=0)


def _unstack_heads(o, n):
    first = _half_mask(LANES)
    pairs = [jnp.where(first, o[(2 * j) * n:(2 * j + 1) * n], o[(2 * j + 1) * n:(2 * j + 2) * n])
             for j in range(SW_GROUP // 2)]
    return jnp.concatenate(pairs, axis=1)


def _sink_column(sink_ref, base, n):
    return jnp.concatenate([jnp.full((n, LANES), sink_ref[base + g], F32) for g in range(SW_GROUP)], axis=0)


def _sw_kernel(sink_ref, q_ref, k_ref, v_ref, kc_ref, vc_ref, o_ref, *, nb, sink_base):
    tq = SW_BLOCK
    gw = SW_GROUP * HEAD_DIM
    row = lax.broadcasted_iota(jnp.int32, (tq, tq), 0)
    col = lax.broadcasted_iota(jnp.int32, (tq, tq), 1)
    band = {-1: jnp.where(col >= row, 0.0, NEG), 1: jnp.where(col <= row, 0.0, NEG)}
    sinks = [_sink_column(sink_ref, sink_base + h * SW_GROUP, tq) for h in range(SW_KV_HEADS)]

    def body(n, carry):
        q0 = pl.multiple_of(n * tq, tq)
        offs = [pl.multiple_of(jnp.clip(n + c, 0, nb - 1) * tq, tq) for c in (-1, 0, 1)]
        edge = []
        for c in (-1, 1):
            inside = jnp.logical_and(n + c >= 0, n + c <= nb - 1)
            edge.append(jnp.where(inside, band[c], NEG))
        mask = jnp.concatenate([jnp.concatenate([edge[0], jnp.zeros_like(edge[0]), edge[1]], axis=1)] * SW_GROUP,
                               axis=0)
        all_scores, all_values = [], []
        for h in range(SW_KV_HEADS):
            lanes = slice(h * LANES, (h + 1) * LANES)
            qs = _stack_heads(q_ref[0, pl.ds(q0, tq), h * gw:(h + 1) * gw])
            k_loc = jnp.concatenate([k_ref[0, pl.ds(off, tq), lanes] for off in offs], axis=0)
            v_loc = jnp.concatenate([v_ref[0, pl.ds(off, tq), lanes] for off in offs], axis=0)
            all_scores.append([_nt(qs, k_loc) + mask, _nt(qs, kc_ref[0, :, lanes])])
            all_values.append([v_loc, vc_ref[0, :, lanes]])
        outs = []
        for h in range(SW_KV_HEADS):
            part = _softmax_parts(all_scores[h], sinks[h])
            outs.append(_unstack_heads(_weighted_values(*part, all_values[h], sinks[h]), tq))
        o_ref[0, pl.ds(q0, tq), :] = jnp.concatenate(outs, axis=1).astype(BF)
        return carry

    lax.fori_loop(0, nb, body, 0)


def _sw_call(q, k, v, kc, vc, sink_flat, layer):
    b, s, w = q.shape
    l = kc.shape[1]
    blk = lambda n, width: pl.BlockSpec((1, n, width), lambda i: (i, 0, 0))
    kvw = SW_KV_HEADS * LANES
    return pl.pallas_call(
        functools.partial(_sw_kernel, nb=s // SW_BLOCK, sink_base=layer * SW_Q_HEADS),
        grid=(b,),
        in_specs=[pl.BlockSpec(memory_space=pltpu.SMEM), blk(s, w), blk(s, kvw), blk(s, kvw),
                  blk(l, kvw), blk(l, kvw)],
        out_specs=blk(s, w),
        out_shape=jax.ShapeDtypeStruct((b, s, w), BF),
        compiler_params=_cparams(1, 40),
        name="window_attn",
    )(sink_flat, q, k, v, kc, vc)


def _ctx_attn_kernel(sink_ref, naq_ref, nak_ref, nav_ref, swq_ref, swk_ref, swv_ref, ob_ref, oc_ref, *, sink_base):
    n = naq_ref.shape[1]
    for p in range(NA_HEADS // 2):
        lanes = slice(p * LANES, (p + 1) * LANES)
        s = _nt(_stack_pair(naq_ref[0, :, lanes]), nak_ref[0, :, lanes])
        ob_ref[0, :, lanes] = _unstack_pair(_softmax_pv([s], [nav_ref[0, :, lanes]]), n).astype(BF)
    gw = SW_GROUP * HEAD_DIM
    for h in range(SW_KV_HEADS):
        lanes = slice(h * LANES, (h + 1) * LANES)
        s = _nt(_stack_heads(swq_ref[0, :, h * gw:(h + 1) * gw]), swk_ref[0, :, lanes])
        sink = _sink_column(sink_ref, sink_base + h * SW_GROUP, n)
        o = _softmax_pv([s], [swv_ref[0, :, lanes]], sink)
        oc_ref[0, :, h * gw:(h + 1) * gw] = _unstack_heads(o, n).astype(BF)


def _ctx_attn_call(naq, nak, nav, swq, swk, swv, sink_flat, layer):
    b, l, w = naq.shape
    blk = lambda width: pl.BlockSpec((1, l, width), lambda i: (i, 0, 0))
    return pl.pallas_call(
        functools.partial(_ctx_attn_kernel, sink_base=layer * SW_Q_HEADS), grid=(b,),
        in_specs=[pl.BlockSpec(memory_space=pltpu.SMEM), blk(w), blk(w), blk(w),
                  blk(w), blk(2 * LANES), blk(2 * LANES)],
        out_specs=[blk(w), blk(w)],
        out_shape=[jax.ShapeDtypeStruct((b, l, w), BF)] * 2,
        compiler_params=_cparams(1, 32),
        name="ctx_attn",
    )(sink_flat, naq, nak, nav, swq, swk, swv)


def _merge_kernel(oa_ref, ob_ref, oc_ref, gate_ref, x_ref, mod_ref, g_ref, wb_ref, wo_ref, wr_ref,
                  x1_ref, h2_ref, aff_ref):
    d = x_ref.shape[2]
    merged = None
    for j, o_ref in enumerate((oa_ref, ob_ref, oc_ref)):
        term = gate_ref[0, :, j * d:(j + 1) * d].astype(F32) * _mm(o_ref[0], wb_ref[j])
        merged = term if merged is None else merged + term
    m = mod_ref[0]
    x1 = x_ref[0] + m[2:3, :] * _mm(merged.astype(BF), wo_ref[...])
    x1_ref[0] = x1
    ms = jnp.mean(x1 * x1, axis=-1, keepdims=True)
    h2 = ((x1 * lax.rsqrt(ms + RMS_EPS)) * g_ref[...]) * (1.0 + m[4:5, :]) + m[3:4, :]
    h2b = h2.astype(BF)
    h2_ref[0] = h2b
    logits = _nt(wr_ref[...], h2b)
    e = jnp.exp(logits - jnp.max(logits, axis=0, keepdims=True))
    aff_ref[0] = e / jnp.sum(e, axis=0, keepdims=True)


def _merge_call(oa, ob, oc, gates, x, mod, layer, mod_row, g, wb, wo, wr_t, *, tm):
    b, n, d = x.shape
    tok = lambda w: pl.BlockSpec((1, tm, w), lambda i, t: (i, t, 0))
    return pl.pallas_call(
        _merge_kernel, grid=(b, n // tm),
        in_specs=[tok(BRANCH), tok(BRANCH), tok(BRANCH), tok(3 * d), tok(d), _mod_spec(layer, mod_row, d),
                  _layer_spec(g, layer), _layer_spec(wb, layer), _layer_spec(wo, layer), _layer_spec(wr_t, layer)],
        out_specs=[tok(d), tok(d), pl.BlockSpec((1, N_EXPERTS, tm), lambda i, t: (i, 0, t))],
        out_shape=[jax.ShapeDtypeStruct((b, n, d), F32), jax.ShapeDtypeStruct((b, n, d), BF),
                   jax.ShapeDtypeStruct((b, N_EXPERTS, n), F32)],
        compiler_params=_cparams(2, 48),
        name="merge_norm_router",
    )(oa, ob, oc, gates, x, mod, g, wb, wo, wr_t)


def _route_consts(nch, seg_rows, cap, gpb):
    r = N_EXPERTS * nch
    idx = np.arange(r)
    seg = idx // seg_rows
    same = seg[:, None] == seg[None, :]
    before = same & (idx[None, :] < idx[:, None])
    lane = np.arange(gpb * LANES)
    same_grp = (lane[:, None] // LANES) == (lane[None, :] // LANES)
    upper = same_grp & (lane[:, None] < lane[None, :])
    base = ((idx % nch) // seg_rows * cap).astype(np.float32).reshape(r, 1)
    return (jnp.asarray(same, BF), jnp.asarray(before, BF), jnp.asarray(upper, BF), jnp.asarray(same_grp, BF),
            jnp.asarray(base))


def _route_kernel(aff_ref, same_ref, before_ref, upper_ref, ones_ref, base_ref, pos_ref, *, cap, gpb):
    bits = jnp.concatenate([lax.bitcast_convert_type(aff_ref[g], jnp.int32) for g in range(gpb)], axis=1)
    same, before, upper, ones = same_ref[...], before_ref[...], upper_ref[...], ones_ref[...]

    def as_mask(cond):
        return jnp.where(cond, 1.0, 0.0).astype(BF)

    def lane_total(col_counts):
        return _mm(col_counts.astype(BF), ones)

    def seg_total(mask_bf):
        return lane_total(_mm(same, mask_bf))

    def prefix(mask_bf):
        return _mm(mask_bf, upper) + lane_total(_mm(before, mask_bf))

    def body(i, thr):
        cand = thr | lax.shift_left(jnp.int32(1), 30 - i)
        return jnp.where(seg_total(as_mask(bits >= cand)) >= cap, cand, thr)

    thr = lax.fori_loop(0, 31, body, jnp.zeros(bits.shape, jnp.int32))
    gt = bits > thr
    eq = bits == thr
    need = cap - seg_total(as_mask(gt))
    sel = jnp.logical_or(gt, jnp.logical_and(eq, prefix(as_mask(eq)) < need))
    slot = jnp.where(sel, prefix(as_mask(sel)) + base_ref[...], -1.0)
    for g in range(gpb):
        pos_ref[g] = slot[:, g * LANES:(g + 1) * LANES]


def _route_call(aff_rows, nch, seg_rows, cap):
    g, r, _ = aff_rows.shape
    gpb = max(n for n in (1, 2, 4) if g % n == 0)
    consts = _route_consts(nch, seg_rows, cap, gpb)
    full = lambda a: pl.BlockSpec(a.shape, lambda i: (0,) * a.ndim)
    return pl.pallas_call(
        functools.partial(_route_kernel, cap=cap, gpb=gpb), grid=(g // gpb,),
        in_specs=[pl.BlockSpec((gpb, r, LANES), lambda i: (i, 0, 0))] + [full(a) for a in consts],
        out_specs=pl.BlockSpec((gpb, r, LANES), lambda i: (i, 0, 0)),
        out_shape=jax.ShapeDtypeStruct((g, r, LANES), F32),
        compiler_params=_cparams(1, 32),
        name="expert_choice_select",
    )(aff_rows, *consts)


def _moe_kernel(h_ref, pos_ref, post_ref, aff_ref, wg_ref, wu_ref, wd_ref, y_ref, *, nslots, rows_per_store):
    e = pl.program_id(1)

    @pl.when(e == 0)
    def _():
        y_ref[...] = jnp.zeros_like(y_ref)

    pos = pos_ref[0, 0]
    aff = aff_ref[0, 0]
    nch = pos.shape[0]
    slot_col = lax.broadcasted_iota(jnp.int32, (nslots, LANES), 0).astype(F32)
    onehot, gate = [], jnp.zeros((nslots, 1), F32)
    for c in range(nch):
        hit = slot_col == pos[c:c + 1, :]
        onehot.append(jnp.where(hit, 1.0, 0.0).astype(BF))
        gate = gate + jnp.sum(jnp.where(hit, aff[c:c + 1, :], 0.0), axis=1, keepdims=True)
    xe = _mm(jnp.concatenate(onehot, axis=1), h_ref[0]).astype(BF)
    a = _mm(xe, wg_ref[0])
    u = _mm(xe, wu_ref[0])
    mid = ((a * jax.nn.sigmoid(a)) * u).astype(BF)
    ye = (_mm(mid, wd_ref[0]) * gate).astype(BF)

    post = post_ref[0, 0]
    slot_row = lax.broadcasted_iota(jnp.int32, (LANES, nslots), 1).astype(F32)
    per = rows_per_store // LANES
    for c0 in range(0, nch, per):
        scat = jnp.concatenate(
            [jnp.where(post[:, c:c + 1] == slot_row, 1.0, 0.0).astype(BF) for c in range(c0, min(c0 + per, nch))],
            axis=0)
        lo = c0 * LANES
        y_ref[0, lo:lo + scat.shape[0], :] += _mm(scat, ye)


def _moe_call(h2, pos, post, aff, wg, wu, wd, layer, nslots):
    g, n, d = h2.shape
    nch = n // LANES
    ff = wg.shape[3]
    tile = lambda: pl.BlockSpec((1, 1, nch, LANES), lambda i, e: (i, e, 0, 0))
    return pl.pallas_call(
        functools.partial(_moe_kernel, nslots=nslots, rows_per_store=min(n, 512)),
        grid=(g, N_EXPERTS),
        in_specs=[pl.BlockSpec((1, n, d), lambda i, e: (i, 0, 0)),
                  tile(), pl.BlockSpec((1, 1, LANES, nch), lambda i, e: (i, e, 0, 0)), tile(),
                  pl.BlockSpec((None, 1, d, ff), lambda i, e: (layer, e, 0, 0)),
                  pl.BlockSpec((None, 1, d, ff), lambda i, e: (layer, e, 0, 0)),
                  pl.BlockSpec((None, 1, ff, d), lambda i, e: (layer, e, 0, 0))],
        out_specs=pl.BlockSpec((1, n, d), lambda i, e: (i, 0, 0)),
        out_shape=jax.ShapeDtypeStruct((g, n, d), F32),
        compiler_params=_cparams(2, 56),
        name="expert_ffn",
    )(h2, pos, post, aff, wg, wu, wd)


def _expert_choice(h2, aff, wg, wu, wd, layer, group):
    b, n, d = h2.shape
    cap = CAPACITY_FACTOR * n // N_EXPERTS
    g = b // group
    ng = group * n
    nch = ng // LANES
    aff_g = aff.reshape(g, group, N_EXPERTS, n).transpose(0, 2, 1, 3).reshape(g, N_EXPERTS * nch, LANES)
    pos = _route_call(aff_g, nch, n // LANES, cap)
    pos4 = pos.reshape(g, N_EXPERTS, nch, LANES)
    y = _moe_call(h2.reshape(g, ng, d), pos4, pos4.transpose(0, 1, 3, 2),
                  aff_g.reshape(g, N_EXPERTS, nch, LANES), wg, wu, wd, layer, group * cap)
    return y.reshape(b, n, d)


def _final_kernel(x_ref, y_ref, mod_ref, g_ref, o_ref):
    x = x_ref[0] + mod_ref[0][5:6, :] * y_ref[0]
    ms = jnp.mean(x * x, axis=-1, keepdims=True)
    o_ref[0] = (x * lax.rsqrt(ms + RMS_EPS)) * g_ref[...]


def _final_call(x, y, mod, layer, g, *, tm):
    b, n, d = x.shape
    tok = pl.BlockSpec((1, tm, d), lambda i, t: (i, t, 0))
    return pl.pallas_call(
        _final_kernel, grid=(b, n // tm),
        in_specs=[tok, tok, _mod_spec(layer, None, d), pl.BlockSpec((1, d), lambda i, t: (0, 0))],
        out_specs=tok, out_shape=jax.ShapeDtypeStruct((b, n, d), F32),
        compiler_params=_cparams(2, 32),
        name="final_norm",
    )(x, y, mod, g.reshape(1, d))


def _rope_tables(s):
    t = np.arange(s)
    row = (t // GRID_W).astype(np.float32)
    col = (t % GRID_W).astype(np.float32)
    nfreq = HEAD_DIM // 4
    inv = (ROPE_BASE ** (-jnp.arange(nfreq, dtype=F32) / nfreq))
    ang = jnp.concatenate([jnp.asarray(row)[:, None] * inv, jnp.asarray(col)[:, None] * inv], axis=-1)
    cos, sin = jnp.cos(ang), jnp.sin(ang)
    cosw = jnp.concatenate([cos, cos] * 2, axis=-1)
    sinw = jnp.concatenate([-sin, sin] * 2, axis=-1)
    return cosw, sinw


def _extend_w_in(w_in):
    k = w_in[..., C_SWK:C_SWK + SW_KV_HEADS * HEAD_DIM]
    v = w_in[..., C_SWK + SW_KV_HEADS * HEAD_DIM:C_SWK + 2 * SW_KV_HEADS * HEAD_DIM]
    dup = lambda a: jnp.concatenate([a[..., :HEAD_DIM]] * 2 + [a[..., HEAD_DIM:]] * 2, axis=-1)
    rest = w_in[..., C_SWK + 2 * SW_KV_HEADS * HEAD_DIM:]
    return jnp.concatenate([w_in[..., :C_SWK], dup(k), dup(v), rest], axis=-1).astype(BF)


def _ctx_group(b, l):
    g = max(1, min(b, 2048 // l))
    while b % g:
        g -= 1
    return g


def kernel(x, c, ctx, c_ctx, w_ada, b_ada, g_mix, w_in, conv_w, na_rpb, sw_sink, w_branch, w_out,
           g_ffn, w_router, w_exp_gate, w_exp_up, w_exp_down, g_final):
    bsz, s, d = x.shape
    l = ctx.shape[1]
    depth = w_ada.shape[0]
    assert d == 1024 and s % (NA_QROWS * GRID_W) == 0 and s // GRID_W >= NA_WIN_ROWS and l % LANES == 0
    tm = 512 if s % 512 == 0 else 256
    tmc = min(256, l)

    mod_rows = -(-(bsz + 1) // 8) * 8
    cs = jnp.zeros((mod_rows, d), F32).at[:bsz].set(c).at[bsz].set(c_ctx)
    mod = _ada_call(cs, w_ada.astype(BF), b_ada).reshape(depth, mod_rows, 6, d)

    w_ext = _extend_w_in(w_in)
    wb, wo = w_branch.astype(BF), w_out.astype(BF)
    wr_t = jnp.swapaxes(w_router, 1, 2).astype(BF)
    wg, wu, wd = w_exp_gate.astype(BF), w_exp_up.astype(BF), w_exp_down.astype(BF)
    g_mix3, g_ffn3 = g_mix.reshape(depth, 1, d), g_ffn.reshape(depth, 1, d)
    sink_flat = sw_sink.reshape(-1)
    cosw, sinw = _rope_tables(s)
    first, cls_of, cls_tiles = _na_layout(s // GRID_W)
    bias = _na_bias(na_rpb, cls_tiles)
    cgroup = _ctx_group(bsz, l)

    xc = ctx
    y = yc = None
    for li in range(depth):
        last = li == depth - 1
        x, (ab, u, naq, nak, nav, swq, swk, swv, gates, nak8, nav8) = _inproj_call(
            x, y, mod, li, None, g_mix3, w_ext, cosw, sinw, tm=tm)
        xc, (cab, cu, cnaq, cnak, cnav, cswq, cswk, cswv, cgates) = _inproj_call(
            xc, yc, mod, li, bsz, g_mix3, w_ext, None, None, tm=tmc)

        o_a = _conv_call(ab, u, conv_w, li)
        o_b = _na_call(naq, nak, nav, nak8, nav8, cnak, cnav, bias, li, first, cls_of)
        o_c = _sw_call(swq, swk, swv, cswk, cswv, sink_flat, li)
        x, h2, aff = _merge_call(o_a, o_b, o_c, gates, x, mod, li, None, g_ffn3, wb, wo, wr_t, tm=tm)
        y = _expert_choice(h2, aff, wg, wu, wd, li, 1)
        if not last:
            co_a = _conv_call(cab, cu, conv_w, li)
            co_b, co_c = _ctx_attn_call(cnaq, cnak, cnav, cswq, cswk, cswv, sink_flat, li)
            xc, h2c, affc = _merge_call(co_a, co_b, co_c, cgates, xc, mod, li, bsz, g_ffn3, wb, wo, wr_t, tm=tmc)
            yc = _expert_choice(h2c, affc, wg, wu, wd, li, cgroup)
    return _final_call(x, y, mod, depth - 1, g_final, tm=tm)
```

```python
import functools

import numpy as np
import jax
import jax.numpy as jnp
from jax import lax
from jax.experimental import pallas as pl
from jax.experimental.pallas import tpu as pltpu

BF = jnp.bfloat16
F32 = jnp.float32

GRID_W = 64
HEAD_DIM = 64
BRANCH = 512
NA_HEADS = 8
NA_ROWS = 8
NA_COLS = 16
NA_QROWS = 4
NA_WIN_ROWS = NA_QROWS + NA_ROWS
SW_Q_HEADS = 8
SW_KV_HEADS = 2
SW_GROUP = SW_Q_HEADS // SW_KV_HEADS
SW_BLOCK = 128
N_EXPERTS = 16
CAPACITY_FACTOR = 2
ROPE_BASE = 10000.0
RMS_EPS = 1e-6
NEG = -1e30
QK_SCALE = HEAD_DIM ** -0.5

LANES = 128
V7X_VMEM_BYTES = 64 * 1024 * 1024

C_AB, C_AC, C_AX = 0, 512, 1024
C_NAQ, C_NAK, C_NAV = 1536, 2048, 2560
C_SWQ, C_SWK, C_SWV = 3072, 3584, 3712
C_GATE, C_END = 3840, 6912


def _cparams(n_axes, vmem_mb):
    return pltpu.CompilerParams(dimension_semantics=("arbitrary",) * n_axes,
                                vmem_limit_bytes=vmem_mb * 1024 * 1024)


def _nt(a, b):
    return lax.dot_general(a, b, (((1,), (1,)), ((), ())), preferred_element_type=F32)


def _mm(a, b):
    return jnp.dot(a, b, preferred_element_type=F32)


def _half_mask(width):
    lane = lax.broadcasted_iota(jnp.int32, (1, width), 1)
    return (lane & (LANES - 1)) < HEAD_DIM


def _lane_blocks(s):
    return [s[:, j:j + LANES] for j in range(0, s.shape[1], LANES)]


def _softmax_pv(scores, values, sink=None):
    return _weighted_values(*_softmax_parts(scores, sink), values, sink)


def _softmax_parts(scores, sink=None, sum_on_vpu=False):
    blocks = [blk for s in scores for blk in _lane_blocks(s)]
    mv = blocks[0]
    for blk in blocks[1:]:
        mv = jnp.maximum(mv, blk)
    shift = jnp.broadcast_to(jnp.max(mv, axis=-1, keepdims=True), mv.shape)
    if sink is not None:
        shift = jnp.maximum(shift, sink)
    probs, folded = [], None
    for s in scores:
        pblocks = []
        for blk in _lane_blocks(s):
            p = jnp.exp(blk - shift)
            if sum_on_vpu:
                folded = p if folded is None else folded + p
            pblocks.append(p.astype(BF))
        probs.append(pblocks[0] if len(pblocks) == 1 else jnp.concatenate(pblocks, axis=1))
    return probs, shift, folded


def _weighted_values(probs, shift, folded, values, sink=None):
    acc = None
    for p, v in zip(probs, values):
        t = _mm(p, v if folded is not None else jnp.concatenate([v, jnp.ones_like(v)], axis=1))
        acc = t if acc is None else acc + t
    total = jnp.sum(folded, axis=-1, keepdims=True) if folded is not None else acc[:, LANES:]
    if sink is not None:
        total = total + jnp.exp(sink - shift)
    return acc[:, :LANES] * (1.0 / total)


def _ada_kernel(cs_ref, w_ref, b_ref, o_ref):
    c = cs_ref[...]
    s = (c * jax.nn.sigmoid(c)).astype(BF)
    o_ref[0] = _mm(s, w_ref[0]) + b_ref[0]


def _ada_call(cs, w_ada_bf, b_ada):
    depth, d, n6 = w_ada_bf.shape
    rows = cs.shape[0]
    tn = 1536
    return pl.pallas_call(
        _ada_kernel,
        grid=(depth, n6 // tn),
        in_specs=[pl.BlockSpec((rows, d), lambda l, j: (0, 0)),
                  pl.BlockSpec((1, d, tn), lambda l, j: (l, 0, j)),
                  pl.BlockSpec((1, 1, tn), lambda l, j: (l, 0, j))],
        out_specs=pl.BlockSpec((1, rows, tn), lambda l, j: (l, 0, j)),
        out_shape=jax.ShapeDtypeStruct((depth, rows, n6), F32),
        compiler_params=_cparams(2, 32),
        name="adaln_mod",
    )(cs, w_ada_bf, b_ada.reshape(depth, 1, n6))


def _rope(z, cos, sin_signed):
    w = z.shape[1]
    rep = w // LANES
    cosw = jnp.concatenate([cos] * rep, axis=1) if rep > 1 else cos
    sinw = jnp.concatenate([sin_signed] * rep, axis=1) if rep > 1 else sin_signed
    lane = lax.broadcasted_iota(jnp.int32, (1, w), 1)
    first = (lane & (HEAD_DIM - 1)) < HEAD_DIM // 2
    partner = jnp.where(first, pltpu.roll(z, w - HEAD_DIM // 2, 1), pltpu.roll(z, HEAD_DIM // 2, 1))
    return z * cosw + partner * sinw


def _inproj_kernel(*refs, has_y, latent):
    it = iter(refs)
    x_ref = next(it)
    if has_y:
        y_ref, modp_ref = next(it), next(it)
    mod_ref, g_ref, w_ref = next(it), next(it), next(it)
    if latent:
        cos_ref, sin_ref = next(it), next(it)
    if has_y:
        xnew_ref = next(it)
    ab_ref, u_ref, naq_ref, nak_ref, nav_ref, swq_ref, swk_ref, swv_ref, gate_ref = (next(it) for _ in range(9))
    if latent:
        nak8_ref, nav8_ref = next(it), next(it)

    x = x_ref[0]
    if has_y:
        x = x + modp_ref[0][5:6, :] * y_ref[0]
        xnew_ref[0] = x
    m = mod_ref[0]
    ms = jnp.mean(x * x, axis=-1, keepdims=True)
    h = ((x * lax.rsqrt(ms + RMS_EPS)) * g_ref[...]) * (1.0 + m[1:2, :]) + m[0:1, :]
    hb = h.astype(BF)

    def proj(lo, hi):
        return _mm(hb, w_ref[:, lo:hi])

    ab_ref[0] = proj(C_AB, C_AC)
    u_ref[0] = proj(C_AC, C_AX) * proj(C_AX, C_NAQ)
    naq_ref[0] = (proj(C_NAQ, C_NAK) * QK_SCALE).astype(BF)
    nk = proj(C_NAK, C_NAV)
    nv = proj(C_NAV, C_SWQ)
    nak_ref[0] = nk.astype(BF)
    nav_ref[0] = nv.astype(BF)
    if latent:
        nak8_ref[0] = jnp.concatenate([nk[NA_SHIFT:], nk[:NA_SHIFT]], axis=0).astype(BF)
        nav8_ref[0] = jnp.concatenate([nv[NA_SHIFT:], nv[:NA_SHIFT]], axis=0).astype(BF)
    q = proj(C_SWQ, C_SWK)
    k = proj(C_SWK, C_SWV)
    if latent:
        cos, sin = cos_ref[...], sin_ref[...]
        q = _rope(q, cos, sin)
        k = _rope(k, cos, sin)
    swq_ref[0] = (q * QK_SCALE).astype(BF)
    swk_ref[0] = _twice_per_head(k).astype(BF)
    swv_ref[0] = _twice_per_head(proj(C_SWV, C_GATE)).astype(BF)
    gate_ref[0] = jax.nn.sigmoid(proj(C_GATE, C_END)).astype(BF)


def _twice_per_head(z):
    swapped = pltpu.roll(z, HEAD_DIM, 1)
    first = _half_mask(LANES)
    return jnp.concatenate([jnp.where(first, z, swapped), jnp.where(first, swapped, z)], axis=1)


def _mod_spec(layer, mod_row, d):
    if mod_row is None:
        return pl.BlockSpec((None, 1, 6, d), lambda i, t: (layer, i, 0, 0))
    return pl.BlockSpec((None, 1, 6, d), lambda i, t: (layer, mod_row, 0, 0))


def _layer_spec(arr, layer):
    zeros = (0,) * (arr.ndim - 1)
    return pl.BlockSpec((None,) + arr.shape[1:], lambda i, t: (layer,) + zeros, pipeline_mode=pl.Buffered(1))


def _inproj_call(x, y, mod, layer, mod_row, g, w_ext, cos, sin, *, tm):
    b, n, d = x.shape
    has_y = y is not None
    latent = cos is not None
    assert not latent or tm % GRID_W == 0
    nt = n // tm
    tok = lambda w: pl.BlockSpec((1, tm, w), lambda i, t: (i, t, 0))
    args, in_specs = [x], [tok(d)]
    if has_y:
        args += [y, mod]
        in_specs += [tok(d), _mod_spec(layer - 1, mod_row, d)]
    args += [mod, g, w_ext]
    in_specs += [_mod_spec(layer, mod_row, d), _layer_spec(g, layer), _layer_spec(w_ext, layer)]
    if latent:
        args += [cos, sin]
        in_specs += [pl.BlockSpec((tm, LANES), lambda i, t: (t, 0))] * 2
    widths = [(BRANCH, F32), (BRANCH, F32), (BRANCH, BF), (BRANCH, BF), (BRANCH, BF),
              (BRANCH, BF), (2 * LANES, BF), (2 * LANES, BF), (3 * d, BF)]
    if latent:
        widths += [(BRANCH, BF), (BRANCH, BF)]
    out_shape = [jax.ShapeDtypeStruct((b, n, w), dt) for w, dt in widths]
    out_specs = [tok(w) for w, _ in widths]
    if has_y:
        out_shape = [jax.ShapeDtypeStruct((b, n, d), F32)] + out_shape
        out_specs = [tok(d)] + out_specs
    outs = pl.pallas_call(
        functools.partial(_inproj_kernel, has_y=has_y, latent=latent),
        grid=(b, nt), in_specs=in_specs, out_specs=out_specs, out_shape=out_shape,
        compiler_params=_cparams(2, 56),
        name="inproj",
    )(*args)
    if has_y:
        return outs[0], outs[1:]
    return x, outs


def _na_layout(rows):
    nblk = rows // NA_QROWS
    first, tiles = [], []
    for i in range(nblk):
        ks = int(np.clip(NA_QROWS * i - NA_ROWS // 2, 0, rows - NA_WIN_ROWS))
        tile = np.full((NA_QROWS, NA_WIN_ROWS), 2 * NA_ROWS - 1, np.int32)
        for rq in range(NA_QROWS):
            r = NA_QROWS * i + rq
            rs = int(np.clip(r - NA_ROWS // 2, 0, rows - NA_ROWS))
            for rk in range(NA_WIN_ROWS):
                ra = ks + rk
                if rs <= ra < rs + NA_ROWS:
                    tile[rq, rk] = ra - r + NA_ROWS - 1
        first.append(ks)
        tiles.append(tile)
    classes, cls_of = [], []
    for tle in tiles:
        for ci, c in enumerate(classes):
            if np.array_equal(c, tle):
                cls_of.append(ci)
                break
        else:
            classes.append(tle)
            cls_of.append(len(classes) - 1)
    return np.array(first, np.int32), np.array(cls_of, np.int32), np.stack(classes)


NA_QCOLS = 16
NA_KCOLS = 32
NA_SHIFT = 8
NA_BLOCKS_PER_ITER = 4
NA_KSTART =tuple(int(np.clip(NA_QCOLS * j - NA_COLS // 2, 0, GRID_W - NA_KCOLS)) for j in range(GRID_W // NA_QCOLS))


def _na_bias(na_rpb, cls_tiles):
    depth, nh = na_rpb.shape[:2]
    qc = np.arange(GRID_W)[:, None]
    kc = np.arange(GRID_W)[None, :]
    toe = np.clip(kc - qc + NA_COLS - 1, 0, 2 * NA_COLS - 2)
    cstart = np.clip(qc - NA_COLS // 2, 0, GRID_W - NA_COLS)
    col_ok = (kc >= cstart) & (kc < cstart + NA_COLS)
    m = jnp.take(na_rpb, jnp.asarray(toe.reshape(-1)), axis=-1)
    m = m.reshape(depth, nh, 2 * NA_ROWS - 1, GRID_W, GRID_W)
    m = jnp.where(jnp.asarray(col_ok), m, NEG)
    m = jnp.concatenate([m, jnp.full((depth, nh, 1, GRID_W, GRID_W), NEG, F32)], axis=2)
    ncls = cls_tiles.shape[0]
    per_block = []
    for j, ks in enumerate(NA_KSTART):
        mj = m[:, :, :, NA_QCOLS * j:NA_QCOLS * (j + 1), ks:ks + NA_KCOLS]
        b = jnp.take(mj, jnp.asarray(cls_tiles.reshape(-1)), axis=2)
        b = b.reshape(depth, nh, ncls, NA_QROWS, NA_WIN_ROWS, NA_QCOLS, NA_KCOLS)
        b = b.transpose(0, 1, 2, 3, 5, 4, 6)
        b = b.reshape(depth, nh // 2, 2, ncls, NA_QROWS * NA_QCOLS, NA_WIN_ROWS * NA_KCOLS)
        per_block.append(b.transpose(0, 1, 3, 2, 4, 5).reshape(
            depth, nh // 2, ncls, 2 * NA_QROWS * NA_QCOLS, NA_WIN_ROWS * NA_KCOLS))
    return jnp.stack(per_block, axis=3)


def _stack_pair(q):
    first = _half_mask(LANES)
    zero = jnp.zeros_like(q)
    return jnp.concatenate([jnp.where(first, q, zero), jnp.where(first, zero, q)], axis=0)


def _unstack_pair(o, n):
    return jnp.where(_half_mask(LANES), o[:n], o[n:])


def _na_kernel(first_ref, cls_ref, q_ref, k_ref, v_ref, k8_ref, v8_ref, kc_ref, vc_ref, bias_ref, o_ref, *, nblk):
    tq = NA_QROWS * GRID_W
    kc = kc_ref[0]
    vc = vc_ref[0]

    def rows(ref, base, step, width, count):
        return jnp.concatenate(
            [ref[0, pl.ds(pl.multiple_of(base + r * GRID_W + step, 2 * NA_SHIFT), width), :] for r in range(count)],
            axis=0)

    def body(it, carry):
        scores, values, dests = [], [], []
        for sub in range(NA_BLOCKS_PER_ITER):
            i = it * NA_BLOCKS_PER_ITER + sub
            k0 = pl.multiple_of(first_ref[i] * GRID_W, tq)
            q0 = pl.multiple_of(i * tq, tq)
            cls = cls_ref[i]
            for j, ks in enumerate(NA_KSTART):
                aligned = ks % (2 * NA_SHIFT) == 0
                ksrc, vsrc = (k_ref, v_ref) if aligned else (k8_ref, v8_ref)
                koff = ks if aligned else ks - NA_SHIFT
                qs = _stack_pair(rows(q_ref, q0, NA_QCOLS * j, NA_QCOLS, NA_QROWS))
                s_loc = _nt(qs, rows(ksrc, k0, koff, NA_KCOLS, NA_WIN_ROWS)) + bias_ref[cls, j]
                scores.append([s_loc, _nt(qs, kc)])
                values.append([rows(vsrc, k0, koff, NA_KCOLS, NA_WIN_ROWS), vc])
                dests.append(q0 + NA_QCOLS * j)
        parts = [_softmax_parts(s, sum_on_vpu=True) for s in scores]
        for part, v, dst in zip(parts, values, dests):
            o = _unstack_pair(_weighted_values(*part, v), NA_QROWS * NA_QCOLS).astype(BF)
            for r in range(NA_QROWS):
                o_ref[0, pl.ds(pl.multiple_of(dst + r * GRID_W, NA_QCOLS), NA_QCOLS), :] = (
                    o[r * NA_QCOLS:(r + 1) * NA_QCOLS])
        return carry

    lax.fori_loop(0, nblk // NA_BLOCKS_PER_ITER, body, 0)


def _na_call(q, k, v, k8, v8, kc, vc, bias, layer, first, cls_of):
    b, s, w = q.shape
    l = kc.shape[1]
    npair = w // LANES
    nblk = s // (NA_QROWS * GRID_W)
    sample = lambda n: pl.BlockSpec((1, n, LANES), lambda p, i, *_: (i, 0, p))
    grid_spec = pltpu.PrefetchScalarGridSpec(
        num_scalar_prefetch=2, grid=(npair, b),
        in_specs=[sample(s)] * 5 + [sample(l), sample(l),
                                    pl.BlockSpec((None, None) + bias.shape[2:],
                                                 lambda p, i, *_: (layer, p, 0, 0, 0, 0))],
        out_specs=sample(s))
    return pl.pallas_call(
        functools.partial(_na_kernel, nblk=nblk), grid_spec=grid_spec,
        out_shape=jax.ShapeDtypeStruct((b, s, w), BF),
        compiler_params=_cparams(2, 48),
        name="nbr_attn",
    )(jnp.asarray(first), jnp.asarray(cls_of), q, k, v, k8, v8, kc, vc, bias)


def _stack_heads(q4):
    first = _half_mask(LANES)
    parts = []
    for g in range(SW_GROUP):
        pair = q4[:, (g // 2) * LANES:(g // 2 + 1) * LANES]
        sel = first if g % 2 == 0 else jnp.logical_not(first)
        parts.append(jnp.where(sel, pair, jnp.zeros_like(pair)))
    return jnp.concatenate(parts, axis=0)


def _unstack_heads(o, n):
    first = _half_mask(LANES)
    pairs = [jnp.where(first, o[(2 * j) * n:(2 * j + 1) * n], o[(2 * j + 1) * n:(2 * j + 2) * n])
             for j in range(SW_GROUP // 2)]
    return jnp.concatenate(pairs, axis=1)


def _sink_column(sink_ref, base, n):
    return jnp.concatenate([jnp.full((n, LANES), sink_ref[base + g], F32) for g in range(SW_GROUP)], axis=0)


def _sw_kernel(sink_ref, q_ref, k_ref, v_ref, kc_ref, vc_ref, o_ref, *, nb, sink_base):
    tq = SW_BLOCK
    gw = SW_GROUP * HEAD_DIM
    row = lax.broadcasted_iota(jnp.int32, (tq, tq), 0)
    col = lax.broadcasted_iota(jnp.int32, (tq, tq), 1)
    band = {-1: jnp.where(col >= row, 0.0, NEG), 1: jnp.where(col <= row, 0.0, NEG)}
    sinks = [_sink_column(sink_ref, sink_base + h * SW_GROUP, tq) for h in range(SW_KV_HEADS)]

    def body(n, carry):
        q0 = pl.multiple_of(n * tq, tq)
        offs = [pl.multiple_of(jnp.clip(n + c, 0, nb - 1) * tq, tq) for c in (-1, 0, 1)]
        edge = []
        for c in (-1, 1):
            inside = jnp.logical_and(n + c >= 0, n + c <= nb - 1)
            edge.append(jnp.where(inside, band[c], NEG))
        mask = jnp.concatenate([jnp.concatenate([edge[0], jnp.zeros_like(edge[0]), edge[1]], axis=1)] * SW_GROUP,
                               axis=0)
        all_scores, all_values = [], []
        for h in range(SW_KV_HEADS):
            lanes = slice(h * LANES, (h + 1) * LANES)
            qs = _stack_heads(q_ref[0, pl.ds(q0, tq), h * gw:(h + 1) * gw])
            k_loc = jnp.concatenate([k_ref[0, pl.ds(off, tq), lanes] for off in offs], axis=0)
            v_loc = jnp.concatenate([v_ref[0, pl.ds(off, tq), lanes] for off in offs], axis=0)
            all_scores.append([_nt(qs, k_loc) + mask, _nt(qs, kc_ref[0, :, lanes])])
            all_values.append([v_loc, vc_ref[0, :, lanes]])
        outs = []
        for h in range(SW_KV_HEADS):
            part = _softmax_parts(all_scores[h], sinks[h])
            outs.append(_unstack_heads(_weighted_values(*part, all_values[h], sinks[h]), tq))
        o_ref[0, pl.ds(q0, tq), :] = jnp.concatenate(outs, axis=1).astype(BF)
        return carry

    lax.fori_loop(0, nb, body, 0, unroll=2)


def _sw_call(q, k, v, kc, vc, sink_flat, layer):
    b, s, w = q.shape
    l = kc.shape[1]
    blk = lambda n, width: pl.BlockSpec((1, n, width), lambda i: (i, 0, 0))
    kvw = SW_KV_HEADS * LANES
    return pl.pallas_call(
        functools.partial(_sw_kernel, nb=s // SW_BLOCK, sink_base=layer * SW_Q_HEADS),
        grid=(b,),
        in_specs=[pl.BlockSpec(memory_space=pltpu.SMEM), blk(s, w), blk(s, kvw), blk(s, kvw),
                  blk(l, kvw), blk(l, kvw)],
        out_specs=blk(s, w),
        out_shape=jax.ShapeDtypeStruct((b, s, w), BF),
        compiler_params=_cparams(1, 40),
        name="window_attn",
    )(sink_flat, q, k, v, kc, vc)


def _ctx_attn_kernel(sink_ref, naq_ref, nak_ref, nav_ref, swq_ref, swk_ref, swv_ref, ob_ref, oc_ref, *, sink_base):
    n = naq_ref.shape[1]
    for p in range(NA_HEADS // 2):
        lanes = slice(p * LANES, (p + 1) * LANES)
        s = _nt(_stack_pair(naq_ref[0, :, lanes]), nak_ref[0, :, lanes])
        ob_ref[0, :, lanes] = _unstack_pair(_softmax_pv([s], [nav_ref[0, :, lanes]]), n).astype(BF)
    gw = SW_GROUP * HEAD_DIM
    for h in range(SW_KV_HEADS):
        lanes = slice(h * LANES, (h + 1) * LANES)
        s = _nt(_stack_heads(swq_ref[0, :, h * gw:(h + 1) * gw]), swk_ref[0, :, lanes])
        sink = _sink_column(sink_ref, sink_base + h * SW_GROUP, n)
        o = _softmax_pv([s], [swv_ref[0, :, lanes]], sink)
        oc_ref[0, :, h * gw:(h + 1) * gw] = _unstack_heads(o, n).astype(BF)


def _ctx_attn_call(naq, nak, nav, swq, swk, swv, sink_flat, layer):
    b, l, w = naq.shape
    blk = lambda width: pl.BlockSpec((1, l, width), lambda i: (i, 0, 0))
    return pl.pallas_call(
        functools.partial(_ctx_attn_kernel, sink_base=layer * SW_Q_HEADS), grid=(b,),
        in_specs=[pl.BlockSpec(memory_space=pltpu.SMEM), blk(w), blk(w), blk(w),
                  blk(w), blk(2 * LANES), blk(2 * LANES)],
        out_specs=[blk(w), blk(w)],
        out_shape=[jax.ShapeDtypeStruct((b, l, w), BF)] * 2,
        compiler_params=_cparams(1, 32),
        name="ctx_attn",
    )(sink_flat, naq, nak, nav, swq, swk, swv)


CONV_HALO = 8


def _short_conv(ab_ref, u_ref, uprev_ref, unext_ref, w_ref):
    u = u_ref[0]
    n = u.shape[0]
    t, nt = pl.program_id(1), pl.num_programs(1)
    row = lax.broadcasted_iota(jnp.int32, (n, 1), 0)
    before = jnp.where(t > 0, uprev_ref[0][CONV_HALO - 1:CONV_HALO, :], 0.0)
    after = jnp.where(t < nt - 1, unext_ref[0][0:1, :], 0.0)
    prev = jnp.where(row == 0, before, pltpu.roll(u, 1, 0))
    nxt = jnp.where(row == n - 1, after, pltpu.roll(u, n - 1, 0))
    w = w_ref[...]
    return (ab_ref[0] * (w[0:1, :] * prev + w[1:2, :] * u + w[2:3, :] * nxt)).astype(BF)


def _merge_kernel(ab_ref, u_ref, uprev_ref, unext_ref, cw_ref, ob_ref, oc_ref, gate_ref, x_ref, mod_ref, g_ref,
                  wb_ref, wo_ref, wr_ref, x1_ref, h2_ref, aff_ref):
    d = x_ref.shape[2]
    branches = (_short_conv(ab_ref, u_ref, uprev_ref, unext_ref, cw_ref), ob_ref[0], oc_ref[0])
    merged = None
    for j, o in enumerate(branches):
        term = gate_ref[0, :, j * d:(j + 1) * d].astype(F32) * _mm(o, wb_ref[j])
        merged = term if merged is None else merged + term
    m = mod_ref[0]
    x1 = x_ref[0] + m[2:3, :] * _mm(merged.astype(BF), wo_ref[...])
    x1_ref[0] = x1
    ms = jnp.mean(x1 * x1, axis=-1, keepdims=True)
    h2 = ((x1 * lax.rsqrt(ms + RMS_EPS)) * g_ref[...]) * (1.0 + m[4:5, :]) + m[3:4, :]
    h2b = h2.astype(BF)
    h2_ref[0] = h2b
    logits = _nt(wr_ref[...], h2b)
    e = jnp.exp(logits - jnp.max(logits, axis=0, keepdims=True))
    aff_ref[0] = e / jnp.sum(e, axis=0, keepdims=True)


def _merge_call(ab, u, conv_w, ob, oc, gates, x, mod, layer, mod_row, g, wb, wo, wr_t, *, tm):
    b, n, d = x.shape
    tok = lambda w: pl.BlockSpec((1, tm, w), lambda i, t: (i, t, 0))
    per, last = tm // CONV_HALO, n // CONV_HALO - 1
    halo_prev = pl.BlockSpec((1, CONV_HALO, BRANCH), lambda i, t: (i, jnp.maximum(t * per - 1, 0), 0))
    halo_next = pl.BlockSpec((1, CONV_HALO, BRANCH), lambda i, t: (i, jnp.minimum((t + 1) * per, last), 0))
    return pl.pallas_call(
        _merge_kernel, grid=(b, n // tm),
        in_specs=[tok(BRANCH), tok(BRANCH), halo_prev, halo_next, _layer_spec(conv_w, layer),
                  tok(BRANCH), tok(BRANCH), tok(3 * d), tok(d), _mod_spec(layer, mod_row, d),
                  _layer_spec(g, layer), _layer_spec(wb, layer), _layer_spec(wo, layer), _layer_spec(wr_t, layer)],
        out_specs=[tok(d), tok(d), pl.BlockSpec((1, N_EXPERTS, tm), lambda i, t: (i, 0, t))],
        out_shape=[jax.ShapeDtypeStruct((b, n, d), F32), jax.ShapeDtypeStruct((b, n, d), BF),
                   jax.ShapeDtypeStruct((b, N_EXPERTS, n), F32)],
        compiler_params=_cparams(2, 48),
        name="merge_norm_router",
    )(ab, u, u, u, conv_w, ob, oc, gates, x, mod, g, wb, wo, wr_t)


def _route_consts(nch, seg_rows, cap, gpb):
    r = N_EXPERTS * nch
    idx = np.arange(r)
    seg = idx // seg_rows
    same = seg[:, None] == seg[None, :]
    before = same & (idx[None, :] < idx[:, None])
    lane = np.arange(gpb * LANES)
    same_grp = (lane[:, None] // LANES) == (lane[None, :] // LANES)
    upper = same_grp & (lane[:, None] < lane[None, :])
    base = ((idx % nch) // seg_rows * cap).astype(np.float32).reshape(r, 1)
    return (jnp.asarray(same, BF), jnp.asarray(before, BF), jnp.asarray(upper, BF), jnp.asarray(same_grp, BF),
            jnp.asarray(base))


def _route_kernel(aff_ref, same_ref, before_ref, upper_ref, ones_ref, base_ref, pos_ref, *, cap, gpb):
    bits = jnp.concatenate([lax.bitcast_convert_type(aff_ref[g], jnp.int32) for g in range(gpb)], axis=1)
    same, before, upper, ones = same_ref[...], before_ref[...], upper_ref[...], ones_ref[...]

    def as_mask(cond):
        return jnp.where(cond, 1.0, 0.0).astype(BF)

    def lane_total(col_counts):
        return _mm(col_counts.astype(BF), ones)

    def seg_total(mask_bf):
        return lane_total(_mm(same, mask_bf))

    def prefix(mask_bf):
        return _mm(mask_bf, upper) + lane_total(_mm(before, mask_bf))

    def body(i, thr):
        cand = thr | lax.shift_left(jnp.int32(1), 30 - i)
        return jnp.where(seg_total(as_mask(bits >= cand)) >= cap, cand, thr)

    thr = lax.fori_loop(0, 31, body, jnp.zeros(bits.shape, jnp.int32))
    gt = bits > thr
    eq = bits == thr
    need = cap - seg_total(as_mask(gt))
    sel = jnp.logical_or(gt, jnp.logical_and(eq, prefix(as_mask(eq)) < need))
    slot = jnp.where(sel, prefix(as_mask(sel)) + base_ref[...], -1.0)
    for g in range(gpb):
        pos_ref[g] = slot[:, g * LANES:(g + 1) * LANES]


def _route_call(aff_rows, nch, seg_rows, cap):
    g, r, _ = aff_rows.shape
    gpb = max(n for n in (1, 2, 4) if g % n == 0)
    consts = _route_consts(nch, seg_rows, cap, gpb)
    full = lambda a: pl.BlockSpec(a.shape, lambda i: (0,) * a.ndim)
    return pl.pallas_call(
        functools.partial(_route_kernel, cap=cap, gpb=gpb), grid=(g // gpb,),
        in_specs=[pl.BlockSpec((gpb, r, LANES), lambda i: (i, 0, 0))] + [full(a) for a in consts],
        out_specs=pl.BlockSpec((gpb, r, LANES), lambda i: (i, 0, 0)),
        out_shape=jax.ShapeDtypeStruct((g, r, LANES), F32),
        compiler_params=_cparams(1, 32),
        name="expert_choice_select",
    )(aff_rows, *consts)


def _moe_kernel(h_ref, pos_ref, post_ref, aff_ref, wg_ref, wu_ref, wd_ref, *rest, nslots, rows_per_store, final):
    if final:
        x_ref, mod_ref, gfin_ref, y_ref = rest
    else:
        (y_ref,) = rest
    e = pl.program_id(1)

    @pl.when(e == 0)
    def _():
        y_ref[...] = jnp.zeros_like(y_ref)

    pos = pos_ref[0, 0]
    aff = aff_ref[0, 0]
    nch = pos.shape[0]
    slot_col = lax.broadcasted_iota(jnp.int32, (nslots, LANES), 0).astype(F32)
    onehot, gate = [], jnp.zeros((nslots, 1), F32)
    for c in range(nch):
        hit = slot_col == pos[c:c + 1, :]
        onehot.append(jnp.where(hit, 1.0, 0.0).astype(BF))
        gate = gate + jnp.sum(jnp.where(hit, aff[c:c + 1, :], 0.0), axis=1, keepdims=True)
    xe = _mm(jnp.concatenate(onehot, axis=1), h_ref[0]).astype(BF)
    a = _mm(xe, wg_ref[0])
    u = _mm(xe, wu_ref[0])
    mid = ((a * jax.nn.sigmoid(a)) * u).astype(BF)
    ye = (_mm(mid, wd_ref[0]) * gate).astype(BF)

    post = post_ref[0, 0]
    slot_row = lax.broadcasted_iota(jnp.int32, (LANES, nslots), 1).astype(F32)
    per = rows_per_store // LANES
    for c0 in range(0, nch, per):
        scat = jnp.concatenate(
            [jnp.where(post[:, c:c + 1] == slot_row, 1.0, 0.0).astype(BF) for c in range(c0, min(c0 + per, nch))],
            axis=0)
        lo = c0 * LANES
        y_ref[0, lo:lo + scat.shape[0], :] += _mm(scat, ye)

    if final:
        @pl.when(e == N_EXPERTS - 1)
        def _():
            gt2 = mod_ref[0][5:6, :]
            for lo in range(0, y_ref.shape[1], rows_per_store):
                rows = slice(lo, lo + rows_per_store)
                x = x_ref[0, rows, :] + gt2 * y_ref[0, rows, :]
                ms = jnp.mean(x * x, axis=-1, keepdims=True)
                y_ref[0, rows, :] = (x * lax.rsqrt(ms + RMS_EPS)) * gfin_ref[...]


def _moe_call(h2, pos, post, aff, wg, wu, wd, layer, nslots, final=None):
    g, n, d = h2.shape
    nch = n // LANES
    ff = wg.shape[3]
    tile = lambda: pl.BlockSpec((1, 1, nch, LANES), lambda i, e: (i, e, 0, 0))
    once = pl.Buffered(1)
    args = [h2, pos, post, aff, wg, wu, wd]
    in_specs = [pl.BlockSpec((1, n, d), lambda i, e: (i, 0, 0), pipeline_mode=once),
                tile(), pl.BlockSpec((1, 1, LANES, nch), lambda i, e: (i, e, 0, 0)), tile(),
                pl.BlockSpec((None, 1, d, ff), lambda i, e: (layer, e, 0, 0)),
                pl.BlockSpec((None, 1, d, ff), lambda i, e: (layer, e, 0, 0)),
                pl.BlockSpec((None, 1, ff, d), lambda i, e: (layer, e, 0, 0))]
    if final is not None:
        x, mod, g_final = final
        args += [x, mod, g_final.reshape(1, d)]
        in_specs += [pl.BlockSpec((1, n, d), lambda i, e: (i, 0, 0), pipeline_mode=once),
                     pl.BlockSpec((None, 1, 6, d), lambda i, e: (layer, i, 0, 0)),
                     pl.BlockSpec((1, d), lambda i, e: (0, 0))]
    return pl.pallas_call(
        functools.partial(_moe_kernel, nslots=nslots, rows_per_store=min(n, 512), final=final is not None),
        grid=(g, N_EXPERTS), in_specs=in_specs,
        out_specs=pl.BlockSpec((1, n, d), lambda i, e: (i, 0, 0)),
        out_shape=jax.ShapeDtypeStruct((g, n, d), F32),
        compiler_params=_cparams(2, 58),
        name="expert_ffn",
    )(*args)


def _expert_choice(h2, aff, wg, wu, wd, layer, group, final=None):
    b, n, d = h2.shape
    assert final is None or group == 1
    cap = CAPACITY_FACTOR * n // N_EXPERTS
    g = b // group
    ng = group * n
    nch = ng // LANES
    aff_g = aff.reshape(g, group, N_EXPERTS, n).transpose(0, 2, 1, 3).reshape(g, N_EXPERTS * nch, LANES)
    pos = _route_call(aff_g, nch, n // LANES, cap)
    pos4 = pos.reshape(g, N_EXPERTS, nch, LANES)
    y = _moe_call(h2.reshape(g, ng, d), pos4, pos4.transpose(0, 1, 3, 2),
                  aff_g.reshape(g, N_EXPERTS, nch, LANES), wg, wu, wd, layer, group * cap, final)
    return y.reshape(b, n, d)


def _rope_tables(s):
    t = np.arange(s)
    row = (t // GRID_W).astype(np.float32)
    col = (t % GRID_W).astype(np.float32)
    nfreq = HEAD_DIM // 4
    inv = (ROPE_BASE ** (-jnp.arange(nfreq, dtype=F32) / nfreq))
    ang = jnp.concatenate([jnp.asarray(row)[:, None] * inv, jnp.asarray(col)[:, None] * inv], axis=-1)
    cos, sin = jnp.cos(ang), jnp.sin(ang)
    cosw = jnp.concatenate([cos, cos] * 2, axis=-1)
    sinw = jnp.concatenate([-sin, sin] * 2, axis=-1)
    return cosw, sinw


def _ctx_group(b, l):
    g = max(1, min(b, 2048 // l))
    while b % g:
        g -= 1
    return g


def kernel(x, c, ctx, c_ctx, w_ada, b_ada, g_mix, w_in, conv_w, na_rpb, sw_sink, w_branch, w_out,
           g_ffn, w_router, w_exp_gate, w_exp_up, w_exp_down, g_final):
    bsz, s, d = x.shape
    l = ctx.shape[1]
    depth = w_ada.shape[0]
    assert d == 1024 and s % (NA_QROWS * GRID_W) == 0 and s // GRID_W >= NA_WIN_ROWS and l % LANES == 0
    tm = 512 if s % 512 == 0 else 256
    tmc = min(256, l)

    mod_rows = -(-(bsz + 1) // 8) * 8
    cs = jnp.zeros((mod_rows, d), F32).at[:bsz].set(c).at[bsz].set(c_ctx)
    mod = _ada_call(cs, w_ada.astype(BF), b_ada).reshape(depth, mod_rows, 6, d)

    assert w_in.shape[-1] == C_END
    w_ext = w_in.astype(BF)
    wb, wo = w_branch.astype(BF), w_out.astype(BF)
    wr_t = jnp.swapaxes(w_router, 1, 2).astype(BF)
    wg, wu, wd = w_exp_gate.astype(BF), w_exp_up.astype(BF), w_exp_down.astype(BF)
    g_mix3, g_ffn3 = g_mix.reshape(depth, 1, d), g_ffn.reshape(depth, 1, d)
    sink_flat = sw_sink.reshape(-1)
    cosw, sinw = _rope_tables(s)
    first, cls_of, cls_tiles = _na_layout(s // GRID_W)
    bias = _na_bias(na_rpb, cls_tiles)
    cgroup = _ctx_group(bsz, l)

    xc = ctx
    y = yc = None
    for li in range(depth):
        last = li == depth - 1
        x, (ab, u, naq, nak, nav, swq, swk, swv, gates, nak8, nav8) = _inproj_call(
            x, y, mod, li, None, g_mix3, w_ext, cosw, sinw, tm=tm)
        xc, (cab, cu, cnaq, cnak, cnav, cswq, cswk, cswv, cgates) = _inproj_call(
            xc, yc, mod, li, bsz, g_mix3, w_ext, None, None, tm=tmc)

        o_b = _na_call(naq, nak, nav, nak8, nav8, cnak, cnav, bias, li, first, cls_of)
        o_c = _sw_call(swq, swk, swv, cswk, cswv, sink_flat, li)
        x, h2, aff = _merge_call(ab, u, conv_w, o_b, o_c, gates, x, mod, li, None, g_ffn3, wb, wo, wr_t, tm=tm)
        y = _expert_choice(h2, aff, wg, wu, wd, li, 1, final=(x, mod, g_final) if last else None)
        if not last:
            co_b, co_c = _ctx_attn_call(cnaq, cnak, cnav, cswq, cswk, cswv, sink_flat, li)
            xc, h2c, affc = _merge_call(cab, cu, conv_w, co_b, co_c, cgates, xc, mod, li, bsz, g_ffn3, wb, wo, wr_t,
                                        tm=tmc)
            yc = _expert_choice(h2c, affc, wg, wu, wd, li, cgroup)
    return y
```

```python
import functools

import numpy as np
import jax
import jax.numpy as jnp
from jax import lax
from jax.experimental import pallas as pl
from jax.experimental.pallas import tpu as pltpu

BF = jnp.bfloat16
F32 = jnp.float32

GRID_W = 64
HEAD_DIM = 64
BRANCH = 512
NA_HEADS = 8
NA_ROWS = 8
NA_COLS = 16
NA_QROWS = 4
NA_WIN_ROWS = NA_QROWS + NA_ROWS
SW_Q_HEADS = 8
SW_KV_HEADS = 2
SW_GROUP = SW_Q_HEADS // SW_KV_HEADS
SW_BLOCK = 128
N_EXPERTS = 16
CAPACITY_FACTOR = 2
ROPE_BASE = 10000.0
RMS_EPS = 1e-6
NEG = -1e30
QK_SCALE = HEAD_DIM ** -0.5

LANES = 128
V7X_VMEM_BYTES = 64 * 1024 * 1024

C_AB, C_AC, C_AX = 0, 512, 1024
C_NAQ, C_NAK, C_NAV = 1536, 2048, 2560
C_SWQ, C_SWK, C_SWV = 3072, 3584, 3712
C_GATE, C_END = 3840, 6912


def _cparams(n_axes, vmem_mb):
    return pltpu.CompilerParams(dimension_semantics=("arbitrary",) * n_axes,
                                vmem_limit_bytes=vmem_mb * 1024 * 1024)


def _nt(a, b):
    return lax.dot_general(a, b, (((1,), (1,)), ((), ())), preferred_element_type=F32)


def _mm(a, b):
    return jnp.dot(a, b, preferred_element_type=F32)


def _half_mask(width):
    lane = lax.broadcasted_iota(jnp.int32, (1, width), 1)
    return (lane & (LANES - 1)) < HEAD_DIM


def _lane_blocks(s):
    return [s[:, j:j + LANES] for j in range(0, s.shape[1], LANES)]


def _softmax_pv(scores, values, sink=None):
    return _weighted_values(*_softmax_parts(scores, sink), values, sink)


def _softmax_parts(scores, sink=None, sum_on_vpu=False):
    blocks = [blk for s in scores for blk in _lane_blocks(s)]
    mv = blocks[0]
    for blk in blocks[1:]:
        mv = jnp.maximum(mv, blk)
    shift = jnp.broadcast_to(jnp.max(mv, axis=-1, keepdims=True), mv.shape)
    if sink is not None:
        shift = jnp.maximum(shift, sink)
    probs, folded = [], None
    for s in scores:
        pblocks = []
        for blk in _lane_blocks(s):
            p = jnp.exp(blk - shift)
            if sum_on_vpu:
                folded = p if folded is None else folded + p
            pblocks.append(p.astype(BF))
        probs.append(pblocks[0] if len(pblocks) == 1 else jnp.concatenate(pblocks, axis=1))
    return probs, shift, folded


def _weighted_values(probs, shift, folded, values, sink=None):
    acc = None
    for p, v in zip(probs, values):
        t = _mm(p, v if folded is not None else jnp.concatenate([v, jnp.ones_like(v)], axis=1))
        acc = t if acc is None else acc + t
    total = jnp.sum(folded, axis=-1, keepdims=True) if folded is not None else acc[:, LANES:]
    if sink is not None:
        total = total + jnp.exp(sink - shift)
    return acc[:, :LANES] * (1.0 / total)


def _ada_kernel(cs_ref, w_ref, b_ref, o_ref):
    c = cs_ref[...]
    s = (c * jax.nn.sigmoid(c)).astype(BF)
    o_ref[0] = _mm(s, w_ref[0]) + b_ref[0]


def _ada_call(cs, w_ada_bf, b_ada):
    depth, d, n6 = w_ada_bf.shape
    rows = cs.shape[0]
    tn = 1536
    return pl.pallas_call(
        _ada_kernel,
        grid=(depth, n6 // tn),
        in_specs=[pl.BlockSpec((rows, d), lambda l, j: (0, 0)),
                  pl.BlockSpec((1, d, tn), lambda l, j: (l, 0, j)),
                  pl.BlockSpec((1, 1, tn), lambda l, j: (l, 0, j))],
        out_specs=pl.BlockSpec((1, rows, tn), lambda l, j: (l, 0, j)),
        out_shape=jax.ShapeDtypeStruct((depth, rows, n6), F32),
        compiler_params=_cparams(2, 32),
        name="adaln_mod",
    )(cs, w_ada_bf, b_ada.reshape(depth, 1, n6))


def _rope(z, cos, sin_signed):
    w = z.shape[1]
    rep = w // LANES
    cosw = jnp.concatenate([cos] * rep, axis=1) if rep > 1 else cos
    sinw = jnp.concatenate([sin_signed] * rep, axis=1) if rep > 1 else sin_signed
    lane = lax.broadcasted_iota(jnp.int32, (1, w), 1)
    first = (lane & (HEAD_DIM - 1)) < HEAD_DIM // 2
    partner = jnp.where(first, pltpu.roll(z, w - HEAD_DIM // 2, 1), pltpu.roll(z, HEAD_DIM // 2, 1))
    return z * cosw + partner * sinw


def _inproj_kernel(*refs, has_y, latent):
    it = iter(refs)
    x_ref = next(it)
    if has_y:
        y_ref, modp_ref = next(it), next(it)
    mod_ref, g_ref, w_ref = next(it), next(it), next(it)
    if latent:
        cos_ref, sin_ref = next(it), next(it)
    if has_y:
        xnew_ref = next(it)
    ab_ref, u_ref, naq_ref, nak_ref, nav_ref, swq_ref, swk_ref, swv_ref, gate_ref = (next(it) for _ in range(9))
    if latent:
        nak8_ref, nav8_ref = next(it), next(it)

    x = x_ref[0]
    if has_y:
        x = x + modp_ref[0][5:6, :] * y_ref[0]
        xnew_ref[0] = x
    m = mod_ref[0]
    ms = jnp.mean(x * x, axis=-1, keepdims=True)
    h = ((x * lax.rsqrt(ms + RMS_EPS)) * g_ref[...]) * (1.0 + m[1:2, :]) + m[0:1, :]
    hb = h.astype(BF)

    def proj(lo, hi):
        return _mm(hb, w_ref[:, lo:hi])

    ab_ref[0] = proj(C_AB, C_AC)
    u_ref[0] = proj(C_AC, C_AX) * proj(C_AX, C_NAQ)
    naq_ref[0] = (proj(C_NAQ, C_NAK) * QK_SCALE).astype(BF)
    nk = proj(C_NAK, C_NAV)
    nv = proj(C_NAV, C_SWQ)
    nak_ref[0] = nk.astype(BF)
    nav_ref[0] = nv.astype(BF)
    if latent:
        nak8_ref[0] = jnp.concatenate([nk[NA_SHIFT:], nk[:NA_SHIFT]], axis=0).astype(BF)
        nav8_ref[0] = jnp.concatenate([nv[NA_SHIFT:], nv[:NA_SHIFT]], axis=0).astype(BF)
    q = proj(C_SWQ, C_SWK)
    k = proj(C_SWK, C_SWV)
    if latent:
        cos, sin = cos_ref[...], sin_ref[...]
        q = _rope(q, cos, sin)
        k = _rope(k, cos, sin)
    swq_ref[0] = (q * QK_SCALE).astype(BF)
    swk_ref[0] = _twice_per_head(k).astype(BF)
    swv_ref[0] = _twice_per_head(proj(C_SWV, C_GATE)).astype(BF)
    gate_ref[0] = jax.nn.sigmoid(proj(C_GATE, C_END)).astype(BF)


def _twice_per_head(z):
    swapped = pltpu.roll(z, HEAD_DIM, 1)
    first = _half_mask(LANES)
    return jnp.concatenate([jnp.where(first, z, swapped), jnp.where(first, swapped, z)], axis=1)


def _mod_spec(layer, mod_row, d):
    if mod_row is None:
        return pl.BlockSpec((None, 1, 6, d), lambda i, t: (layer, i, 0, 0))
    return pl.BlockSpec((None, 1, 6, d), lambda i, t: (layer, mod_row, 0, 0))


def _layer_spec(arr, layer):
    zeros = (0,) * (arr.ndim - 1)
    return pl.BlockSpec((None,) + arr.shape[1:], lambda i, t: (layer,) + zeros, pipeline_mode=pl.Buffered(1))


def _inproj_call(x, y, mod, layer, mod_row, g, w_ext, cos, sin, *, tm):
    b, n, d = x.shape
    has_y = y is not None
    latent = cos is not None
    assert not latent or tm % GRID_W == 0
    nt = n // tm
    tok = lambda w: pl.BlockSpec((1, tm, w), lambda i, t: (i, t, 0))
    args, in_specs = [x], [tok(d)]
    if has_y:
        args += [y, mod]
        in_specs += [tok(d), _mod_spec(layer - 1, mod_row, d)]
    args += [mod, g, w_ext]
    in_specs += [_mod_spec(layer, mod_row, d), _layer_spec(g, layer), _layer_spec(w_ext, layer)]
    if latent:
        args += [cos, sin]
        in_specs += [pl.BlockSpec((tm, LANES), lambda i, t: (t, 0))] * 2
    widths = [(BRANCH, F32), (BRANCH, F32), (BRANCH, BF), (BRANCH, BF), (BRANCH, BF),
              (BRANCH, BF), (2 * LANES, BF), (2 * LANES, BF), (3 * d, BF)]
    if latent:
        widths += [(BRANCH, BF), (BRANCH, BF)]
    out_shape = [jax.ShapeDtypeStruct((b, n, w), dt) for w, dt in widths]
    out_specs = [tok(w) for w, _ in widths]
    if has_y:
        out_shape = [jax.ShapeDtypeStruct((b, n, d), F32)] + out_shape
        out_specs = [tok(d)] + out_specs
    outs = pl.pallas_call(
        functools.partial(_inproj_kernel, has_y=has_y, latent=latent),
        grid=(b, nt), in_specs=in_specs, out_specs=out_specs, out_shape=out_shape,
        compiler_params=_cparams(2, 56),
        name="inproj",
    )(*args)
    if has_y:
        return outs[0], outs[1:]
    return x, outs


def _na_layout(rows):
    nblk = rows // NA_QROWS
    first, tiles = [], []
    for i in range(nblk):
        ks = int(np.clip(NA_QROWS * i - NA_ROWS // 2, 0, rows - NA_WIN_ROWS))
        tile = np.full((NA_QROWS, NA_WIN_ROWS), 2 * NA_ROWS - 1, np.int32)
        for rq in range(NA_QROWS):
            r = NA_QROWS * i + rq
            rs = int(np.clip(r - NA_ROWS // 2, 0, rows - NA_ROWS))
            for rk in range(NA_WIN_ROWS):
                ra = ks + rk
                if rs <= ra < rs + NA_ROWS:
                    tile[rq, rk] = ra - r + NA_ROWS - 1
        first.append(ks)
        tiles.append(tile)
    classes, cls_of = [], []
    for tle in tiles:
        for ci, c in enumerate(classes):
            if np.array_equal(c, tle):
                cls_of.append(ci)
                break
        else:
            classes.append(tle)
            cls_of.append(len(classes) - 1)
    return np.array(first, np.int32), np.array(cls_of, np.int32), np.stack(classes)


NA_QCOLS = 16
NA_KCOLS = 32
NA_SHIFT = 8
NA_BLOCKS_PER_ITER = 4
NA_KSTART =tuple(int(np.clip(NA_QCOLS * j - NA_COLS // 2, 0, GRID_W - NA_KCOLS)) for j in range(GRID_W // NA_QCOLS))


def _na_bias(na_rpb, cls_tiles):
    depth, nh = na_rpb.shape[:2]
    qc = np.arange(GRID_W)[:, None]
    kc = np.arange(GRID_W)[None, :]
    toe = np.clip(kc - qc + NA_COLS - 1, 0, 2 * NA_COLS - 2)
    cstart = np.clip(qc - NA_COLS // 2, 0, GRID_W - NA_COLS)
    col_ok = (kc >= cstart) & (kc < cstart + NA_COLS)
    m = jnp.take(na_rpb, jnp.asarray(toe.reshape(-1)), axis=-1)
    m = m.reshape(depth, nh, 2 * NA_ROWS - 1, GRID_W, GRID_W)
    m = jnp.where(jnp.asarray(col_ok), m, NEG)
    m = jnp.concatenate([m, jnp.full((depth, nh, 1, GRID_W, GRID_W), NEG, F32)], axis=2)
    ncls = cls_tiles.shape[0]
    per_block = []
    for j, ks in enumerate(NA_KSTART):
        mj = m[:, :, :, NA_QCOLS * j:NA_QCOLS * (j + 1), ks:ks + NA_KCOLS]
        b = jnp.take(mj, jnp.asarray(cls_tiles.reshape(-1)), axis=2)
        b = b.reshape(depth, nh, ncls, NA_QROWS, NA_WIN_ROWS, NA_QCOLS, NA_KCOLS)
        b = b.transpose(0, 1, 2, 3, 5, 4, 6)
        b = b.reshape(depth, nh // 2, 2, ncls, NA_QROWS * NA_QCOLS, NA_WIN_ROWS * NA_KCOLS)
        per_block.append(b.transpose(0, 1, 3, 2, 4, 5).reshape(
            depth, nh // 2, ncls, 2 * NA_QROWS * NA_QCOLS, NA_WIN_ROWS * NA_KCOLS))
    return jnp.stack(per_block, axis=3)


def _stack_pair(q):
    first = _half_mask(LANES)
    zero = jnp.zeros_like(q)
    return jnp.concatenate([jnp.where(first, q, zero), jnp.where(first, zero, q)], axis=0)


def _unstack_pair(o, n):
    return jnp.where(_half_mask(LANES), o[:n], o[n:])


def _na_kernel(first_ref, cls_ref, q_ref, k_ref, v_ref, k8_ref, v8_ref, kc_ref, vc_ref, bias_ref, o_ref, *, nblk):
    tq = NA_QROWS * GRID_W
    kc = kc_ref[0]
    vc = vc_ref[0]

    def rows(ref, base, step, width, count):
        return jnp.concatenate(
            [ref[0, pl.ds(pl.multiple_of(base + r * GRID_W + step, 2 * NA_SHIFT), width), :] for r in range(count)],
            axis=0)

    def body(it, carry):
        scores, values, dests = [], [], []
        for sub in range(NA_BLOCKS_PER_ITER):
            i = it * NA_BLOCKS_PER_ITER + sub
            k0 = pl.multiple_of(first_ref[i] * GRID_W, tq)
            q0 = pl.multiple_of(i * tq, tq)
            cls = cls_ref[i]
            for j, ks in enumerate(NA_KSTART):
                aligned = ks % (2 * NA_SHIFT) == 0
                ksrc, vsrc = (k_ref, v_ref) if aligned else (k8_ref, v8_ref)
                koff = ks if aligned else ks - NA_SHIFT
                qs = _stack_pair(rows(q_ref, q0, NA_QCOLS * j, NA_QCOLS, NA_QROWS))
                s_loc = _nt(qs, rows(ksrc, k0, koff, NA_KCOLS, NA_WIN_ROWS)) + bias_ref[cls, j]
                scores.append([s_loc, _nt(qs, kc)])
                values.append([rows(vsrc, k0, koff, NA_KCOLS, NA_WIN_ROWS), vc])
                dests.append(q0 + NA_QCOLS * j)
        parts = [_softmax_parts(s, sum_on_vpu=True) for s in scores]
        for part, v, dst in zip(parts, values, dests):
            o = _unstack_pair(_weighted_values(*part, v), NA_QROWS * NA_QCOLS).astype(BF)
            for r in range(NA_QROWS):
                o_ref[0, pl.ds(pl.multiple_of(dst + r * GRID_W, NA_QCOLS), NA_QCOLS), :] = (
                    o[r * NA_QCOLS:(r + 1) * NA_QCOLS])
        return carry

    lax.fori_loop(0, nblk // NA_BLOCKS_PER_ITER, body, 0)


def _na_call(q, k, v, k8, v8, kc, vc, bias, layer, first, cls_of):
    b, s, w = q.shape
    l = kc.shape[1]
    npair = w // LANES
    nblk = s // (NA_QROWS * GRID_W)
    sample = lambda n: pl.BlockSpec((1, n, LANES), lambda p, i, *_: (i, 0, p))
    grid_spec = pltpu.PrefetchScalarGridSpec(
        num_scalar_prefetch=2, grid=(npair, b),
        in_specs=[sample(s)] * 5 + [sample(l), sample(l),
                                    pl.BlockSpec((None, None) + bias.shape[2:],
                                                 lambda p, i, *_: (layer, p, 0, 0, 0, 0))],
        out_specs=sample(s))
    return pl.pallas_call(
        functools.partial(_na_kernel, nblk=nblk), grid_spec=grid_spec,
        out_shape=jax.ShapeDtypeStruct((b, s, w), BF),
        compiler_params=_cparams(2, 48),
        name="nbr_attn",
    )(jnp.asarray(first), jnp.asarray(cls_of), q, k, v, k8, v8, kc, vc, bias)


def _stack_heads(q4):
    first = _half_mask(LANES)
    parts = []
    for g in range(SW_GROUP):
        pair = q4[:, (g // 2) * LANES:(g // 2 + 1) * LANES]
        sel = first if g % 2 == 0 else jnp.logical_not(first)
        parts.append(jnp.where(sel, pair, jnp.zeros_like(pair)))
    return jnp.concatenate(parts, axis=0)


def _unstack_heads(o, n):
    first = _half_mask(LANES)
    pairs = [jnp.where(first, o[(2 * j) * n:(2 * j + 1) * n], o[(2 * j + 1) * n:(2 * j + 2) * n])
             for j in range(SW_GROUP // 2)]
    return jnp.concatenate(pairs, axis=1)


def _sink_column(sink_ref, base, n):
    return jnp.concatenate([jnp.full((n, LANES), sink_ref[base + g], F32) for g in range(SW_GROUP)], axis=0)


def _sw_kernel(sink_ref, q_ref, k_ref, v_ref, kc_ref, vc_ref, o_ref, *, nb, sink_base):
    tq = SW_BLOCK
    gw = SW_GROUP * HEAD_DIM
    row = lax.broadcasted_iota(jnp.int32, (tq, tq), 0)
    col = lax.broadcasted_iota(jnp.int32, (tq, tq), 1)
    band = {-1: jnp.where(col >= row, 0.0, NEG), 1: jnp.where(col <= row, 0.0, NEG)}
    sinks = [_sink_column(sink_ref, sink_base + h * SW_GROUP, tq) for h in range(SW_KV_HEADS)]

    def body(n, carry):
        q0 = pl.multiple_of(n * tq, tq)
        offs = [pl.multiple_of(jnp.clip(n + c, 0, nb - 1) * tq, tq) for c in (-1, 0, 1)]
        edge = []
        for c in (-1, 1):
            inside = jnp.logical_and(n + c >= 0, n + c <= nb - 1)
            edge.append(jnp.where(inside, band[c], NEG))
        mask = jnp.concatenate([jnp.concatenate([edge[0], jnp.zeros_like(edge[0]), edge[1]], axis=1)] * SW_GROUP,
                               axis=0)
        all_scores, all_values = [], []
        for h in range(SW_KV_HEADS):
            lanes = slice(h * LANES, (h + 1) * LANES)
            qs = _stack_heads(q_ref[0, pl.ds(q0, tq), h * gw:(h + 1) * gw])
            k_loc = jnp.concatenate([k_ref[0, pl.ds(off, tq), lanes] for off in offs], axis=0)
            v_loc = jnp.concatenate([v_ref[0, pl.ds(off, tq), lanes] for off in offs], axis=0)
            all_scores.append([_nt(qs, k_loc) + mask, _nt(qs, kc_ref[0, :, lanes])])
            all_values.append([v_loc, vc_ref[0, :, lanes]])
        outs = []
        for h in range(SW_KV_HEADS):
            part = _softmax_parts(all_scores[h], sinks[h])
            outs.append(_unstack_heads(_weighted_values(*part, all_values[h], sinks[h]), tq))
        o_ref[0, pl.ds(q0, tq), :] = jnp.concatenate(outs, axis=1).astype(BF)
        return carry

    lax.fori_loop(0, nb, body, 0, unroll=2)


def _sw_call(q, k, v, kc, vc, sink_flat, layer):
    b, s, w = q.shape
    l = kc.shape[1]
    blk = lambda n, width: pl.BlockSpec((1, n, width), lambda i: (i, 0, 0))
    kvw = SW_KV_HEADS * LANES
    return pl.pallas_call(
        functools.partial(_sw_kernel, nb=s // SW_BLOCK, sink_base=layer * SW_Q_HEADS),
        grid=(b,),
        in_specs=[pl.BlockSpec(memory_space=pltpu.SMEM), blk(s, w), blk(s, kvw), blk(s, kvw),
                  blk(l, kvw), blk(l, kvw)],
        out_specs=blk(s, w),
        out_shape=jax.ShapeDtypeStruct((b, s, w), BF),
        compiler_params=_cparams(1, 40),
        name="window_attn",
    )(sink_flat, q, k, v, kc, vc)


def _ctx_attn_kernel(sink_ref, naq_ref, nak_ref, nav_ref, swq_ref, swk_ref, swv_ref, ob_ref, oc_ref, *, sink_base):
    n = naq_ref.shape[1]
    for p in range(NA_HEADS // 2):
        lanes = slice(p * LANES, (p + 1) * LANES)
        s = _nt(_stack_pair(naq_ref[0, :, lanes]), nak_ref[0, :, lanes])
        ob_ref[0, :, lanes] = _unstack_pair(_softmax_pv([s], [nav_ref[0, :, lanes]]), n).astype(BF)
    gw = SW_GROUP * HEAD_DIM
    for h in range(SW_KV_HEADS):
        lanes = slice(h * LANES, (h + 1) * LANES)
        s = _nt(_stack_heads(swq_ref[0, :, h * gw:(h + 1) * gw]), swk_ref[0, :, lanes])
        sink = _sink_column(sink_ref, sink_base + h * SW_GROUP, n)
        o = _softmax_pv([s], [swv_ref[0, :, lanes]], sink)
        oc_ref[0, :, h * gw:(h + 1) * gw] = _unstack_heads(o, n).astype(BF)


def _ctx_attn_call(naq, nak, nav, swq, swk, swv, sink_flat, layer):
    b, l, w = naq.shape
    blk = lambda width: pl.BlockSpec((1, l, width), lambda i: (i, 0, 0))
    return pl.pallas_call(
        functools.partial(_ctx_attn_kernel, sink_base=layer * SW_Q_HEADS), grid=(b,),
        in_specs=[pl.BlockSpec(memory_space=pltpu.SMEM), blk(w), blk(w), blk(w),
                  blk(w), blk(2 * LANES), blk(2 * LANES)],
        out_specs=[blk(w), blk(w)],
        out_shape=[jax.ShapeDtypeStruct((b, l, w), BF)] * 2,
        compiler_params=_cparams(1, 32),
        name="ctx_attn",
    )(sink_flat, naq, nak, nav, swq, swk, swv)


CONV_HALO = 8


def _short_conv(ab_ref, u_ref, uprev_ref, unext_ref, w_ref):
    u = u_ref[0]
    n = u.shape[0]
    t, nt = pl.program_id(1), pl.num_programs(1)
    row = lax.broadcasted_iota(jnp.int32, (n, 1), 0)
    before = jnp.where(t > 0, uprev_ref[0][CONV_HALO - 1:CONV_HALO, :], 0.0)
    after = jnp.where(t < nt - 1, unext_ref[0][0:1, :], 0.0)
    prev = jnp.where(row == 0, before, pltpu.roll(u, 1, 0))
    nxt = jnp.where(row == n - 1, after, pltpu.roll(u, n - 1, 0))
    w = w_ref[...]
    return (ab_ref[0] * (w[0:1, :] * prev + w[1:2, :] * u + w[2:3, :] * nxt)).astype(BF)


def _merge_kernel(ab_ref, u_ref, uprev_ref, unext_ref, cw_ref, ob_ref, oc_ref, gate_ref, x_ref, mod_ref, g_ref,
                  wb_ref, wo_ref, wr_ref, x1_ref, h2_ref, aff_ref):
    d = x_ref.shape[2]
    branches = (_short_conv(ab_ref, u_ref, uprev_ref, unext_ref, cw_ref), ob_ref[0], oc_ref[0])
    merged = None
    for j, o in enumerate(branches):
        term = gate_ref[0, :, j * d:(j + 1) * d].astype(F32) * _mm(o, wb_ref[j])
        merged = term if merged is None else merged + term
    m = mod_ref[0]
    x1 = x_ref[0] + m[2:3, :] * _mm(merged.astype(BF), wo_ref[...])
    x1_ref[0] = x1
    ms = jnp.mean(x1 * x1, axis=-1, keepdims=True)
    h2 = ((x1 * lax.rsqrt(ms + RMS_EPS)) * g_ref[...]) * (1.0 + m[4:5, :]) + m[3:4, :]
    h2b = h2.astype(BF)
    h2_ref[0] = h2b
    logits = _nt(wr_ref[...], h2b)
    e = jnp.exp(logits - jnp.max(logits, axis=0, keepdims=True))
    aff_ref[0] = e / jnp.sum(e, axis=0, keepdims=True)


def _merge_call(ab, u, conv_w, ob, oc, gates, x, mod, layer, mod_row, g, wb, wo, wr_t, *, tm):
    b, n, d = x.shape
    tok = lambda w: pl.BlockSpec((1, tm, w), lambda i, t: (i, t, 0))
    per, last = tm // CONV_HALO, n // CONV_HALO - 1
    halo_prev = pl.BlockSpec((1, CONV_HALO, BRANCH), lambda i, t: (i, jnp.maximum(t * per - 1, 0), 0))
    halo_next = pl.BlockSpec((1, CONV_HALO, BRANCH), lambda i, t: (i, jnp.minimum((t + 1) * per, last), 0))
    return pl.pallas_call(
        _merge_kernel, grid=(b, n // tm),
        in_specs=[tok(BRANCH), tok(BRANCH), halo_prev, halo_next, _layer_spec(conv_w, layer),
                  tok(BRANCH), tok(BRANCH), tok(3 * d), tok(d), _mod_spec(layer, mod_row, d),
                  _layer_spec(g, layer), _layer_spec(wb, layer), _layer_spec(wo, layer), _layer_spec(wr_t, layer)],
        out_specs=[tok(d), tok(d), pl.BlockSpec((1, N_EXPERTS, tm), lambda i, t: (i, 0, t))],
        out_shape=[jax.ShapeDtypeStruct((b, n, d), F32), jax.ShapeDtypeStruct((b, n, d), BF),
                   jax.ShapeDtypeStruct((b, N_EXPERTS, n), F32)],
        compiler_params=_cparams(2, 48),
        name="merge_norm_router",
    )(ab, u, u, u, conv_w, ob, oc, gates, x, mod, g, wb, wo, wr_t)


def _route_consts(nch, seg_rows, cap, gpb):
    r = N_EXPERTS * nch
    idx = np.arange(r)
    seg = idx // seg_rows
    same = seg[:, None] == seg[None, :]
    before = same & (idx[None, :] < idx[:, None])
    lane = np.arange(gpb * LANES)
    same_grp = (lane[:, None] // LANES) == (lane[None, :] // LANES)
    upper = same_grp & (lane[:, None] < lane[None, :])
    base = ((idx % nch) // seg_rows * cap).astype(np.float32).reshape(r, 1)
    return (jnp.asarray(same, BF), jnp.asarray(before, BF), jnp.asarray(upper, BF), jnp.asarray(same_grp, BF),
            jnp.asarray(base))


def _route_kernel(aff_ref, same_ref, before_ref, upper_ref, ones_ref, base_ref, pos_ref, *, cap, gpb):
    bits = jnp.concatenate([lax.bitcast_convert_type(aff_ref[g], jnp.int32) for g in range(gpb)], axis=1)
    same, before, upper, ones = same_ref[...], before_ref[...], upper_ref[...], ones_ref[...]

    def as_mask(cond):
        return jnp.where(cond, 1.0, 0.0).astype(BF)

    def lane_total(col_counts):
        return _mm(col_counts.astype(BF), ones)

    def seg_total(mask_bf):
        return lane_total(_mm(same, mask_bf))

    def prefix(mask_bf):
        return _mm(mask_bf, upper) + lane_total(_mm(before, mask_bf))

    def body(i, thr):
        cand = thr | lax.shift_left(jnp.int32(1), 30 - i)
        return jnp.where(seg_total(as_mask(bits >= cand)) >= cap, cand, thr)

    thr = lax.fori_loop(0, 31, body, jnp.zeros(bits.shape, jnp.int32))
    gt = bits > thr
    eq = bits == thr
    need = cap - seg_total(as_mask(gt))
    sel = jnp.logical_or(gt, jnp.logical_and(eq, prefix(as_mask(eq)) < need))
    slot = jnp.where(sel, prefix(as_mask(sel)) + base_ref[...], -1.0)
    for g in range(gpb):
        pos_ref[g] = slot[:, g * LANES:(g + 1) * LANES]


def _route_call(aff_rows, nch, seg_rows, cap):
    g, r, _ = aff_rows.shape
    gpb = max(n for n in (1, 2, 4) if g % n == 0)
    consts = _route_consts(nch, seg_rows, cap, gpb)
    full = lambda a: pl.BlockSpec(a.shape, lambda i: (0,) * a.ndim)
    return pl.pallas_call(
        functools.partial(_route_kernel, cap=cap, gpb=gpb), grid=(g // gpb,),
        in_specs=[pl.BlockSpec((gpb, r, LANES), lambda i: (i, 0, 0))] + [full(a) for a in consts],
        out_specs=pl.BlockSpec((gpb, r, LANES), lambda i: (i, 0, 0)),
        out_shape=jax.ShapeDtypeStruct((g, r, LANES), F32),
        compiler_params=_cparams(1, 32),
        name="expert_choice_select",
    )(aff_rows, *consts)


def _moe_kernel(h_ref, pos_ref, post_ref, aff_ref, wg_ref, wu_ref, wd_ref, *rest, nslots, rows_per_store, final):
    if final:
        x_ref, mod_ref, gfin_ref, y_ref, x_all = rest
    else:
        (y_ref,) = rest
    e = pl.program_id(1)
    if final:
        xrows = x_ref.shape[1]
        x_all[pl.ds(pl.multiple_of(e * xrows, xrows), xrows), :] = x_ref[0]

    @pl.when(e == 0)
    def _():
        y_ref[...] = jnp.zeros_like(y_ref)

    pos = pos_ref[0, 0]
    aff = aff_ref[0, 0]
    nch = pos.shape[0]
    slot_col = lax.broadcasted_iota(jnp.int32, (nslots, LANES), 0).astype(F32)
    onehot, gate = [], jnp.zeros((nslots, 1), F32)
    for c in range(nch):
        hit = slot_col == pos[c:c + 1, :]
        onehot.append(jnp.where(hit, 1.0, 0.0).astype(BF))
        gate = gate + jnp.sum(jnp.where(hit, aff[c:c + 1, :], 0.0), axis=1, keepdims=True)
    xe = _mm(jnp.concatenate(onehot, axis=1), h_ref[0]).astype(BF)
    a = _mm(xe, wg_ref[0])
    u = _mm(xe, wu_ref[0])
    mid = ((a * jax.nn.sigmoid(a)) * u).astype(BF)
    ye = (_mm(mid, wd_ref[0]) * gate).astype(BF)

    post = post_ref[0, 0]
    slot_row = lax.broadcasted_iota(jnp.int32, (LANES, nslots), 1).astype(F32)
    per = rows_per_store // LANES
    for c0 in range(0, nch, per):
        scat = jnp.concatenate(
            [jnp.where(post[:, c:c + 1] == slot_row, 1.0, 0.0).astype(BF) for c in range(c0, min(c0 + per, nch))],
            axis=0)
        lo = c0 * LANES
        y_ref[0, lo:lo + scat.shape[0], :] += _mm(scat, ye)

    if final:
        @pl.when(e == N_EXPERTS - 1)
        def _():
            gt2 = mod_ref[0][5:6, :]
            for lo in range(0, y_ref.shape[1], rows_per_store):
                rows = slice(lo, lo + rows_per_store)
                x = x_all[rows, :] + gt2 * y_ref[0, rows, :]
                ms = jnp.mean(x * x, axis=-1, keepdims=True)
                y_ref[0, rows, :] = (x * lax.rsqrt(ms + RMS_EPS)) * gfin_ref[...]


def _moe_call(h2, pos, post, aff, wg, wu, wd, layer, nslots, final=None):
    g, n, d = h2.shape
    nch = n // LANES
    ff = wg.shape[3]
    tile = lambda: pl.BlockSpec((1, 1, nch, LANES), lambda i, e: (i, e, 0, 0))
    args = [h2, pos, post, aff, wg, wu, wd]
    in_specs = [pl.BlockSpec((1, n, d), lambda i, e: (i, 0, 0)),
                tile(), pl.BlockSpec((1, 1, LANES, nch), lambda i, e: (i, e, 0, 0)), tile(),
                pl.BlockSpec((None, 1, d, ff), lambda i, e: (layer, e, 0, 0)),
                pl.BlockSpec((None, 1, d, ff), lambda i, e: (layer, e, 0, 0)),
                pl.BlockSpec((None, 1, ff, d), lambda i, e: (layer, e, 0, 0))]
    scratch = []
    if final is not None:
        x, mod, g_final = final
        assert n % N_EXPERTS == 0
        args += [x, mod, g_final.reshape(1, d)]
        in_specs += [pl.BlockSpec((1, n // N_EXPERTS, d), lambda i, e: (i, e, 0)),
                     pl.BlockSpec((None, 1, 6, d), lambda i, e: (layer, i, 0, 0)),
                     pl.BlockSpec((1, d), lambda i, e: (0, 0))]
        scratch = [pltpu.VMEM((n, d), F32)]
    return pl.pallas_call(
        functools.partial(_moe_kernel, nslots=nslots, rows_per_store=min(n, 512), final=final is not None),
        grid=(g, N_EXPERTS), in_specs=in_specs,
        out_specs=pl.BlockSpec((1, n, d), lambda i, e: (i, 0, 0)),
        out_shape=jax.ShapeDtypeStruct((g, n, d), F32),
        scratch_shapes=scratch,
        compiler_params=_cparams(2, 58),
        name="expert_ffn",
    )(*args)


def _expert_choice(h2, aff, wg, wu, wd, layer, group, final=None):
    b, n, d = h2.shape
    assert final is None or group == 1
    cap = CAPACITY_FACTOR * n // N_EXPERTS
    g = b // group
    ng = group * n
    nch = ng // LANES
    aff_g = aff.reshape(g, group, N_EXPERTS, n).transpose(0, 2, 1, 3).reshape(g, N_EXPERTS * nch, LANES)
    pos = _route_call(aff_g, nch, n // LANES, cap)
    pos4 = pos.reshape(g, N_EXPERTS, nch, LANES)
    y = _moe_call(h2.reshape(g, ng, d), pos4, pos4.transpose(0, 1, 3, 2),
                  aff_g.reshape(g, N_EXPERTS, nch, LANES), wg, wu, wd, layer, group * cap, final)
    return y.reshape(b, n, d)


def _rope_tables(s):
    t = np.arange(s)
    row = (t // GRID_W).astype(np.float32)
    col = (t % GRID_W).astype(np.float32)
    nfreq = HEAD_DIM // 4
    inv = (ROPE_BASE ** (-jnp.arange(nfreq, dtype=F32) / nfreq))
    ang = jnp.concatenate([jnp.asarray(row)[:, None] * inv, jnp.asarray(col)[:, None] * inv], axis=-1)
    cos, sin = jnp.cos(ang), jnp.sin(ang)
    cosw = jnp.concatenate([cos, cos] * 2, axis=-1)
    sinw = jnp.concatenate([-sin, sin] * 2, axis=-1)
    return cosw, sinw


def _ctx_group(b, l):
    g = max(1, min(b, 2048 // l))
    while b % g:
        g -= 1
    return g


def kernel(x, c, ctx, c_ctx, w_ada, b_ada, g_mix, w_in, conv_w, na_rpb, sw_sink, w_branch, w_out,
           g_ffn, w_router, w_exp_gate, w_exp_up, w_exp_down, g_final):
    bsz, s, d = x.shape
    l = ctx.shape[1]
    depth = w_ada.shape[0]
    assert d == 1024 and s % (NA_QROWS * GRID_W) == 0 and s // GRID_W >= NA_WIN_ROWS and l % LANES == 0
    tm = 512 if s % 512 == 0 else 256
    tmc = min(256, l)

    mod_rows = -(-(bsz + 1) // 8) * 8
    cs = jnp.zeros((mod_rows, d), F32).at[:bsz].set(c).at[bsz].set(c_ctx)
    mod = _ada_call(cs, w_ada.astype(BF), b_ada).reshape(depth, mod_rows, 6, d)

    assert w_in.shape[-1] == C_END
    w_ext = w_in.astype(BF)
    wb, wo = w_branch.astype(BF), w_out.astype(BF)
    wr_t = jnp.swapaxes(w_router, 1, 2).astype(BF)
    wg, wu, wd = w_exp_gate.astype(BF), w_exp_up.astype(BF), w_exp_down.astype(BF)
    g_mix3, g_ffn3 = g_mix.reshape(depth, 1, d), g_ffn.reshape(depth, 1, d)
    sink_flat = sw_sink.reshape(-1)
    cosw, sinw = _rope_tables(s)
    first, cls_of, cls_tiles = _na_layout(s // GRID_W)
    bias = _na_bias(na_rpb, cls_tiles)
    cgroup = _ctx_group(bsz, l)

    xc = ctx
    y = yc = None
    for li in range(depth):
        last = li == depth - 1
        x, (ab, u, naq, nak, nav, swq, swk, swv, gates, nak8, nav8) = _inproj_call(
            x, y, mod, li, None, g_mix3, w_ext, cosw, sinw, tm=tm)
        xc, (cab, cu, cnaq, cnak, cnav, cswq, cswk, cswv, cgates) = _inproj_call(
            xc, yc, mod, li, bsz, g_mix3, w_ext, None, None, tm=tmc)

        o_b = _na_call(naq, nak, nav, nak8, nav8, cnak, cnav, bias, li, first, cls_of)
        o_c = _sw_call(swq, swk, swv, cswk, cswv, sink_flat, li)
        x, h2, aff = _merge_call(ab, u, conv_w, o_b, o_c, gates, x, mod, li, None, g_ffn3, wb, wo, wr_t, tm=tm)
        y = _expert_choice(h2, aff, wg, wu, wd, li, 1, final=(x, mod, g_final) if last else None)
        if not last:
            co_b, co_c = _ctx_attn_call(cnaq, cnak, cnav, cswq, cswk, cswv, sink_flat, li)
            xc, h2c, affc = _merge_call(cab, cu, conv_w, co_b, co_c, cgates, xc, mod, li, bsz, g_ffn3, wb, wo, wr_t,
                                        tm=tmc)
            yc = _expert_choice(h2c, affc, wg, wu, wd, li, cgroup)
    return y
```

```python
import functools

import numpy as np
import jax
import jax.numpy as jnp
from jax import lax
from jax.experimental import pallas as pl
from jax.experimental.pallas import tpu as pltpu

BF = jnp.bfloat16
F32 = jnp.float32

GRID_W = 64
HEAD_DIM = 64
BRANCH = 512
NA_HEADS = 8
NA_ROWS = 8
NA_COLS = 16
NA_QROWS = 4
NA_WIN_ROWS = NA_QROWS + NA_ROWS
SW_Q_HEADS = 8
SW_KV_HEADS = 2
SW_GROUP = SW_Q_HEADS // SW_KV_HEADS
SW_BLOCK = 128
N_EXPERTS = 16
CAPACITY_FACTOR = 2
ROPE_BASE = 10000.0
RMS_EPS = 1e-6
NEG = -1e30
QK_SCALE = HEAD_DIM ** -0.5

LANES = 128
V7X_VMEM_BYTES = 64 * 1024 * 1024

C_AB, C_AC, C_AX = 0, 512, 1024
C_NAQ, C_NAK, C_NAV = 1536, 2048, 2560
C_SWQ, C_SWK, C_SWV = 3072, 3584, 3712
C_GATE, C_END = 3840, 6912


def _cparams(n_axes, vmem_mb):
    return pltpu.CompilerParams(dimension_semantics=("arbitrary",) * n_axes,
                                vmem_limit_bytes=vmem_mb * 1024 * 1024)


def _nt(a, b):
    return lax.dot_general(a, b, (((1,), (1,)), ((), ())), preferred_element_type=F32)


def _mm(a, b):
    return jnp.dot(a, b, preferred_element_type=F32)


def _half_mask(width):
    lane = lax.broadcasted_iota(jnp.int32, (1, width), 1)
    return (lane & (LANES - 1)) < HEAD_DIM


def _lane_blocks(s):
    return [s[:, j:j + LANES] for j in range(0, s.shape[1], LANES)]


def _softmax_pv(scores, values, sink=None):
    return _weighted_values(*_softmax_parts(scores, sink), values, sink)


def _softmax_parts(scores, sink=None, sum_on_vpu=False):
    blocks = [blk for s in scores for blk in _lane_blocks(s)]
    mv = blocks[0]
    for blk in blocks[1:]:
        mv = jnp.maximum(mv, blk)
    shift = jnp.broadcast_to(jnp.max(mv, axis=-1, keepdims=True), mv.shape)
    if sink is not None:
        shift = jnp.maximum(shift, sink)
    probs, folded = [], None
    for s in scores:
        pblocks = []
        for blk in _lane_blocks(s):
            p = jnp.exp(blk - shift)
            if sum_on_vpu:
                folded = p if folded is None else folded + p
            pblocks.append(p.astype(BF))
        probs.append(pblocks[0] if len(pblocks) == 1 else jnp.concatenate(pblocks, axis=1))
    return probs, shift, folded


def _weighted_values(probs, shift, folded, values, sink=None):
    acc = None
    for p, v in zip(probs, values):
        t = _mm(p, v if folded is not None else jnp.concatenate([v, jnp.ones_like(v)], axis=1))
        acc = t if acc is None else acc + t
    total = jnp.sum(folded, axis=-1, keepdims=True) if folded is not None else acc[:, LANES:]
    if sink is not None:
        total = total + jnp.exp(sink - shift)
    return acc[:, :LANES] * (1.0 / total)


def _ada_kernel(cs_ref, w_ref, b_ref, o_ref):
    c = cs_ref[...]
    s = (c * jax.nn.sigmoid(c)).astype(BF)
    o_ref[0] = _mm(s, w_ref[0]) + b_ref[0]


def _ada_call(cs, w_ada_bf, b_ada):
    depth, d, n6 = w_ada_bf.shape
    rows = cs.shape[0]
    tn = 1536
    return pl.pallas_call(
        _ada_kernel,
        grid=(depth, n6 // tn),
        in_specs=[pl.BlockSpec((rows, d), lambda l, j: (0, 0)),
                  pl.BlockSpec((1, d, tn), lambda l, j: (l, 0, j)),
                  pl.BlockSpec((1, 1, tn), lambda l, j: (l, 0, j))],
        out_specs=pl.BlockSpec((1, rows, tn), lambda l, j: (l, 0, j)),
        out_shape=jax.ShapeDtypeStruct((depth, rows, n6), F32),
        compiler_params=_cparams(2, 32),
        name="adaln_mod",
    )(cs, w_ada_bf, b_ada.reshape(depth, 1, n6))


def _rope(z, cos, sin_signed):
    w = z.shape[1]
    rep = w // LANES
    cosw = jnp.concatenate([cos] * rep, axis=1) if rep > 1 else cos
    sinw = jnp.concatenate([sin_signed] * rep, axis=1) if rep > 1 else sin_signed
    lane = lax.broadcasted_iota(jnp.int32, (1, w), 1)
    first = (lane & (HEAD_DIM - 1)) < HEAD_DIM // 2
    partner = jnp.where(first, pltpu.roll(z, w - HEAD_DIM // 2, 1), pltpu.roll(z, HEAD_DIM // 2, 1))
    return z * cosw + partner * sinw


def _inproj_kernel(*refs, has_y, latent):
    it = iter(refs)
    x_ref = next(it)
    if has_y:
        y_ref, modp_ref = next(it), next(it)
    mod_ref, g_ref, w_ref = next(it), next(it), next(it)
    if latent:
        cos_ref, sin_ref = next(it), next(it)
    if has_y:
        xnew_ref = next(it)
    ab_ref, u_ref, naq_ref, nak_ref, nav_ref, swq_ref, swk_ref, swv_ref, gate_ref = (next(it) for _ in range(9))
    if latent:
        nak8_ref, nav8_ref = next(it), next(it)

    x = x_ref[0]
    if has_y:
        x = x + modp_ref[0][5:6, :] * y_ref[0]
        xnew_ref[0] = x
    m = mod_ref[0]
    ms = jnp.mean(x * x, axis=-1, keepdims=True)
    h = ((x * lax.rsqrt(ms + RMS_EPS)) * g_ref[...]) * (1.0 + m[1:2, :]) + m[0:1, :]
    hb = h.astype(BF)

    def proj(lo, hi):
        return _mm(hb, w_ref[:, lo:hi])

    ab_ref[0] = proj(C_AB, C_AC)
    u_ref[0] = proj(C_AC, C_AX) * proj(C_AX, C_NAQ)
    naq_ref[0] = (proj(C_NAQ, C_NAK) * QK_SCALE).astype(BF)
    nk = proj(C_NAK, C_NAV)
    nv = proj(C_NAV, C_SWQ)
    nak_ref[0] = nk.astype(BF)
    nav_ref[0] = nv.astype(BF)
    if latent:
        nak8_ref[0] = jnp.concatenate([nk[NA_SHIFT:], nk[:NA_SHIFT]], axis=0).astype(BF)
        nav8_ref[0] = jnp.concatenate([nv[NA_SHIFT:], nv[:NA_SHIFT]], axis=0).astype(BF)
    q = proj(C_SWQ, C_SWK)
    k = proj(C_SWK, C_SWV)
    if latent:
        cos, sin = cos_ref[...], sin_ref[...]
        q = _rope(q, cos, sin)
        k = _rope(k, cos, sin)
    swq_ref[0] = (q * QK_SCALE).astype(BF)
    swk_ref[0] = _twice_per_head(k).astype(BF)
    swv_ref[0] = _twice_per_head(proj(C_SWV, C_GATE)).astype(BF)
    gate_ref[0] = jax.nn.sigmoid(proj(C_GATE, C_END)).astype(BF)


def _twice_per_head(z):
    swapped = pltpu.roll(z, HEAD_DIM, 1)
    first = _half_mask(LANES)
    return jnp.concatenate([jnp.where(first, z, swapped), jnp.where(first, swapped, z)], axis=1)


def _mod_spec(layer, mod_row, d):
    if mod_row is None:
        return pl.BlockSpec((None, 1, 6, d), lambda i, t: (layer, i, 0, 0))
    return pl.BlockSpec((None, 1, 6, d), lambda i, t: (layer, mod_row, 0, 0))


def _layer_spec(arr, layer):
    zeros = (0,) * (arr.ndim - 1)
    return pl.BlockSpec((None,) + arr.shape[1:], lambda i, t: (layer,) + zeros, pipeline_mode=pl.Buffered(1))


def _inproj_call(x, y, mod, layer, mod_row, g, w_ext, cos, sin, *, tm):
    b, n, d = x.shape
    has_y = y is not None
    latent = cos is not None
    assert not latent or tm % GRID_W == 0
    nt = n // tm
    tok = lambda w: pl.BlockSpec((1, tm, w), lambda i, t: (i, t, 0))
    args, in_specs = [x], [tok(d)]
    if has_y:
        args += [y, mod]
        in_specs += [tok(d), _mod_spec(layer - 1, mod_row, d)]
    args += [mod, g, w_ext]
    in_specs += [_mod_spec(layer, mod_row, d), _layer_spec(g, layer), _layer_spec(w_ext, layer)]
    if latent:
        args += [cos, sin]
        in_specs += [pl.BlockSpec((tm, LANES), lambda i, t: (t, 0))] * 2
    widths = [(BRANCH, F32), (BRANCH, F32), (BRANCH, BF), (BRANCH, BF), (BRANCH, BF),
              (BRANCH, BF), (2 * LANES, BF), (2 * LANES, BF), (3 * d, BF)]
    if latent:
        widths += [(BRANCH, BF), (BRANCH, BF)]
    out_shape = [jax.ShapeDtypeStruct((b, n, w), dt) for w, dt in widths]
    out_specs = [tok(w) for w, _ in widths]
    if has_y:
        out_shape = [jax.ShapeDtypeStruct((b, n, d), F32)] + out_shape
        out_specs = [tok(d)] + out_specs
    outs = pl.pallas_call(
        functools.partial(_inproj_kernel, has_y=has_y, latent=latent),
        grid=(b, nt), in_specs=in_specs, out_specs=out_specs, out_shape=out_shape,
        compiler_params=_cparams(2, 56),
        name="inproj",
    )(*args)
    if has_y:
        return outs[0], outs[1:]
    return x, outs


def _na_layout(rows):
    nblk = rows // NA_QROWS
    first, tiles = [], []
    for i in range(nblk):
        ks = int(np.clip(NA_QROWS * i - NA_ROWS // 2, 0, rows - NA_WIN_ROWS))
        tile = np.full((NA_QROWS, NA_WIN_ROWS), 2 * NA_ROWS - 1, np.int32)
        for rq in range(NA_QROWS):
            r = NA_QROWS * i + rq
            rs = int(np.clip(r - NA_ROWS // 2, 0, rows - NA_ROWS))
            for rk in range(NA_WIN_ROWS):
                ra = ks + rk
                if rs <= ra < rs + NA_ROWS:
                    tile[rq, rk] = ra - r + NA_ROWS - 1
        first.append(ks)
        tiles.append(tile)
    classes, cls_of = [], []
    for tle in tiles:
        for ci, c in enumerate(classes):
            if np.array_equal(c, tle):
                cls_of.append(ci)
                break
        else:
            classes.append(tle)
            cls_of.append(len(classes) - 1)
    return np.array(first, np.int32), np.array(cls_of, np.int32), np.stack(classes)


NA_QCOLS = 16
NA_KCOLS = 32
NA_SHIFT = 8
NA_BLOCKS_PER_ITER = 4
NA_KSTART =tuple(int(np.clip(NA_QCOLS * j - NA_COLS // 2, 0, GRID_W - NA_KCOLS)) for j in range(GRID_W // NA_QCOLS))


def _na_bias(na_rpb, cls_tiles):
    depth, nh = na_rpb.shape[:2]
    qc = np.arange(GRID_W)[:, None]
    kc = np.arange(GRID_W)[None, :]
    toe = np.clip(kc - qc + NA_COLS - 1, 0, 2 * NA_COLS - 2)
    cstart = np.clip(qc - NA_COLS // 2, 0, GRID_W - NA_COLS)
    col_ok = (kc >= cstart) & (kc < cstart + NA_COLS)
    m = jnp.take(na_rpb, jnp.asarray(toe.reshape(-1)), axis=-1)
    m = m.reshape(depth, nh, 2 * NA_ROWS - 1, GRID_W, GRID_W)
    m = jnp.where(jnp.asarray(col_ok), m, NEG)
    m = jnp.concatenate([m, jnp.full((depth, nh, 1, GRID_W, GRID_W), NEG, F32)], axis=2)
    ncls = cls_tiles.shape[0]
    per_block = []
    for j, ks in enumerate(NA_KSTART):
        mj = m[:, :, :, NA_QCOLS * j:NA_QCOLS * (j + 1), ks:ks + NA_KCOLS]
        b = jnp.take(mj, jnp.asarray(cls_tiles.reshape(-1)), axis=2)
        b = b.reshape(depth, nh, ncls, NA_QROWS, NA_WIN_ROWS, NA_QCOLS, NA_KCOLS)
        b = b.transpose(0, 1, 2, 3, 5, 4, 6)
        b = b.reshape(depth, nh // 2, 2, ncls, NA_QROWS * NA_QCOLS, NA_WIN_ROWS * NA_KCOLS)
        per_block.append(b.transpose(0, 1, 3, 2, 4, 5).reshape(
            depth, nh // 2, ncls, 2 * NA_QROWS * NA_QCOLS, NA_WIN_ROWS * NA_KCOLS))
    return jnp.stack(per_block, axis=3)


def _stack_pair(q):
    first = _half_mask(LANES)
    zero = jnp.zeros_like(q)
    return jnp.concatenate([jnp.where(first, q, zero), jnp.where(first, zero, q)], axis=0)


def _unstack_pair(o, n):
    return jnp.where(_half_mask(LANES), o[:n], o[n:])


def _na_kernel(first_ref, cls_ref, q_ref, k_ref, v_ref, k8_ref, v8_ref, kc_ref, vc_ref, bias_ref, o_ref, *, nblk):
    tq = NA_QROWS * GRID_W
    kc = kc_ref[0]
    vc = vc_ref[0]

    def rows(ref, base, step, width, count):
        return jnp.concatenate(
            [ref[0, pl.ds(pl.multiple_of(base + r * GRID_W + step, 2 * NA_SHIFT), width), :] for r in range(count)],
            axis=0)

    def body(it, carry):
        scores, values, dests = [], [], []
        for sub in range(NA_BLOCKS_PER_ITER):
            i = it * NA_BLOCKS_PER_ITER + sub
            k0 = pl.multiple_of(first_ref[i] * GRID_W, tq)
            q0 = pl.multiple_of(i * tq, tq)
            cls = cls_ref[i]
            for j, ks in enumerate(NA_KSTART):
                aligned = ks % (2 * NA_SHIFT) == 0
                ksrc, vsrc = (k_ref, v_ref) if aligned else (k8_ref, v8_ref)
                koff = ks if aligned else ks - NA_SHIFT
                qs = _stack_pair(rows(q_ref, q0, NA_QCOLS * j, NA_QCOLS, NA_QROWS))
                s_loc = _nt(qs, rows(ksrc, k0, koff, NA_KCOLS, NA_WIN_ROWS)) + bias_ref[cls, j]
                scores.append([s_loc, _nt(qs, kc)])
                values.append([rows(vsrc, k0, koff, NA_KCOLS, NA_WIN_ROWS), vc])
                dests.append(q0 + NA_QCOLS * j)
        parts = [_softmax_parts(s, sum_on_vpu=True) for s in scores]
        for part, v, dst in zip(parts, values, dests):
            o = _unstack_pair(_weighted_values(*part, v), NA_QROWS * NA_QCOLS).astype(BF)
            for r in range(NA_QROWS):
                o_ref[0, pl.ds(pl.multiple_of(dst + r * GRID_W, NA_QCOLS), NA_QCOLS), :] = (
                    o[r * NA_QCOLS:(r + 1) * NA_QCOLS])
        return carry

    lax.fori_loop(0, nblk // NA_BLOCKS_PER_ITER, body, 0)


def _na_call(q, k, v, k8, v8, kc, vc, bias, layer, first, cls_of):
    b, s, w = q.shape
    l = kc.shape[1]
    npair = w // LANES
    nblk = s // (NA_QROWS * GRID_W)
    sample = lambda n: pl.BlockSpec((1, n, LANES), lambda p, i, *_: (i, 0, p))
    grid_spec = pltpu.PrefetchScalarGridSpec(
        num_scalar_prefetch=2, grid=(npair, b),
        in_specs=[sample(s)] * 5 + [sample(l), sample(l),
                                    pl.BlockSpec((None, None) + bias.shape[2:],
                                                 lambda p, i, *_: (layer, p, 0, 0, 0, 0))],
        out_specs=sample(s))
    return pl.pallas_call(
        functools.partial(_na_kernel, nblk=nblk), grid_spec=grid_spec,
        out_shape=jax.ShapeDtypeStruct((b, s, w), BF),
        compiler_params=_cparams(2, 48),
        name="nbr_attn",
    )(jnp.asarray(first), jnp.asarray(cls_of), q, k, v, k8, v8, kc, vc, bias)


def _stack_heads(q4):
    first = _half_mask(LANES)
    parts = []
    for g in range(SW_GROUP):
        pair = q4[:, (g // 2) * LANES:(g // 2 + 1) * LANES]
        sel = first if g % 2 == 0 else jnp.logical_not(first)
        parts.append(jnp.where(sel, pair, jnp.zeros_like(pair)))
    return jnp.concatenate(parts, axis=0)


def _unstack_heads(o, n):
    first = _half_mask(LANES)
    pairs = [jnp.where(first, o[(2 * j) * n:(2 * j + 1) * n], o[(2 * j + 1) * n:(2 * j + 2) * n])
             for j in range(SW_GROUP // 2)]
    return jnp.concatenate(pairs, axis=1)


def _sink_column(sink_ref, base, n):
    return jnp.concatenate([jnp.full((n, LANES), sink_ref[base + g], F32) for g in range(SW_GROUP)], axis=0)


def _sw_kernel(sink_ref, q_ref, k_ref, v_ref, kc_ref, vc_ref, o_ref, *, nb, sink_base):
    tq = SW_BLOCK
    gw = SW_GROUP * HEAD_DIM
    row = lax.broadcasted_iota(jnp.int32, (tq, tq), 0)
    col = lax.broadcasted_iota(jnp.int32, (tq, tq), 1)
    band = {-1: jnp.where(col >= row, 0.0, NEG), 1: jnp.where(col <= row, 0.0, NEG)}
    sinks = [_sink_column(sink_ref, sink_base + h * SW_GROUP, tq) for h in range(SW_KV_HEADS)]

    def body(n, carry):
        q0 = pl.multiple_of(n * tq, tq)
        offs = [pl.multiple_of(jnp.clip(n + c, 0, nb - 1) * tq, tq) for c in (-1, 0, 1)]
        edge = []
        for c in (-1, 1):
            inside = jnp.logical_and(n + c >= 0, n + c <= nb - 1)
            edge.append(jnp.where(inside, band[c], NEG))
        mask = jnp.concatenate([jnp.concatenate([edge[0], jnp.zeros_like(edge[0]), edge[1]], axis=1)] * SW_GROUP,
                               axis=0)
        all_scores, all_values = [], []
        for h in range(SW_KV_HEADS):
            lanes = slice(h * LANES, (h + 1) * LANES)
            qs = _stack_heads(q_ref[0, pl.ds(q0, tq), h * gw:(h + 1) * gw])
            k_loc = jnp.concatenate([k_ref[0, pl.ds(off, tq), lanes] for off in offs], axis=0)
            v_loc = jnp.concatenate([v_ref[0, pl.ds(off, tq), lanes] for off in offs], axis=0)
            all_scores.append([_nt(qs, k_loc) + mask, _nt(qs, kc_ref[0, :, lanes])])
            all_values.append([v_loc, vc_ref[0, :, lanes]])
        outs = []
        for h in range(SW_KV_HEADS):
            part = _softmax_parts(all_scores[h], sinks[h])
            outs.append(_unstack_heads(_weighted_values(*part, all_values[h], sinks[h]), tq))
        o_ref[0, pl.ds(q0, tq), :] = jnp.concatenate(outs, axis=1).astype(BF)
        return carry

    lax.fori_loop(0, nb, body, 0, unroll=4)


def _sw_call(q, k, v, kc, vc, sink_flat, layer):
    b, s, w = q.shape
    l = kc.shape[1]
    blk = lambda n, width: pl.BlockSpec((1, n, width), lambda i: (i, 0, 0))
    kvw = SW_KV_HEADS * LANES
    return pl.pallas_call(
        functools.partial(_sw_kernel, nb=s // SW_BLOCK, sink_base=layer * SW_Q_HEADS),
        grid=(b,),
        in_specs=[pl.BlockSpec(memory_space=pltpu.SMEM), blk(s, w), blk(s, kvw), blk(s, kvw),
                  blk(l, kvw), blk(l, kvw)],
        out_specs=blk(s, w),
        out_shape=jax.ShapeDtypeStruct((b, s, w), BF),
        compiler_params=_cparams(1, 40),
        name="window_attn",
    )(sink_flat, q, k, v, kc, vc)


def _ctx_attn_kernel(sink_ref, naq_ref, nak_ref, nav_ref, swq_ref, swk_ref, swv_ref, ob_ref, oc_ref, *, sink_base):
    n = naq_ref.shape[1]
    for p in range(NA_HEADS // 2):
        lanes = slice(p * LANES, (p + 1) * LANES)
        s = _nt(_stack_pair(naq_ref[0, :, lanes]), nak_ref[0, :, lanes])
        ob_ref[0, :, lanes] = _unstack_pair(_softmax_pv([s], [nav_ref[0, :, lanes]]), n).astype(BF)
    gw = SW_GROUP * HEAD_DIM
    for h in range(SW_KV_HEADS):
        lanes = slice(h * LANES, (h + 1) * LANES)
        s = _nt(_stack_heads(swq_ref[0, :, h * gw:(h + 1) * gw]), swk_ref[0, :, lanes])
        sink = _sink_column(sink_ref, sink_base + h * SW_GROUP, n)
        o = _softmax_pv([s], [swv_ref[0, :, lanes]], sink)
        oc_ref[0, :, h * gw:(h + 1) * gw] = _unstack_heads(o, n).astype(BF)


def _ctx_attn_call(naq, nak, nav, swq, swk, swv, sink_flat, layer):
    b, l, w = naq.shape
    blk = lambda width: pl.BlockSpec((1, l, width), lambda i: (i, 0, 0))
    return pl.pallas_call(
        functools.partial(_ctx_attn_kernel, sink_base=layer * SW_Q_HEADS), grid=(b,),
        in_specs=[pl.BlockSpec(memory_space=pltpu.SMEM), blk(w), blk(w), blk(w),
                  blk(w), blk(2 * LANES), blk(2 * LANES)],
        out_specs=[blk(w), blk(w)],
        out_shape=[jax.ShapeDtypeStruct((b, l, w), BF)] * 2,
        compiler_params=_cparams(1, 32),
        name="ctx_attn",
    )(sink_flat, naq, nak, nav, swq, swk, swv)


CONV_HALO = 8


def _short_conv(ab_ref, u_ref, uprev_ref, unext_ref, w_ref):
    u = u_ref[0]
    n = u.shape[0]
    t, nt = pl.program_id(1), pl.num_programs(1)
    row = lax.broadcasted_iota(jnp.int32, (n, 1), 0)
    before = jnp.where(t > 0, uprev_ref[0][CONV_HALO - 1:CONV_HALO, :], 0.0)
    after = jnp.where(t < nt - 1, unext_ref[0][0:1, :], 0.0)
    prev = jnp.where(row == 0, before, pltpu.roll(u, 1, 0))
    nxt = jnp.where(row == n - 1, after, pltpu.roll(u, n - 1, 0))
    w = w_ref[...]
    return (ab_ref[0] * (w[0:1, :] * prev + w[1:2, :] * u + w[2:3, :] * nxt)).astype(BF)


def _merge_kernel(ab_ref, u_ref, uprev_ref, unext_ref, cw_ref, ob_ref, oc_ref, gate_ref, x_ref, mod_ref, g_ref,
                  wb_ref, wo_ref, wr_ref, x1_ref, h2_ref, aff_ref):
    d = x_ref.shape[2]
    branches = (_short_conv(ab_ref, u_ref, uprev_ref, unext_ref, cw_ref), ob_ref[0], oc_ref[0])
    merged = None
    for j, o in enumerate(branches):
        term = gate_ref[0, :, j * d:(j + 1) * d].astype(F32) * _mm(o, wb_ref[j])
        merged = term if merged is None else merged + term
    m = mod_ref[0]
    x1 = x_ref[0] + m[2:3, :] * _mm(merged.astype(BF), wo_ref[...])
    x1_ref[0] = x1
    ms = jnp.mean(x1 * x1, axis=-1, keepdims=True)
    h2 = ((x1 * lax.rsqrt(ms + RMS_EPS)) * g_ref[...]) * (1.0 + m[4:5, :]) + m[3:4, :]
    h2b = h2.astype(BF)
    h2_ref[0] = h2b
    logits = _nt(wr_ref[...], h2b)
    e = jnp.exp(logits - jnp.max(logits, axis=0, keepdims=True))
    aff_ref[0] = e / jnp.sum(e, axis=0, keepdims=True)


def _merge_call(ab, u, conv_w, ob, oc, gates, x, mod, layer, mod_row, g, wb, wo, wr_t, *, tm):
    b, n, d = x.shape
    tok = lambda w: pl.BlockSpec((1, tm, w), lambda i, t: (i, t, 0))
    per, last = tm // CONV_HALO, n // CONV_HALO - 1
    halo_prev = pl.BlockSpec((1, CONV_HALO, BRANCH), lambda i, t: (i, jnp.maximum(t * per - 1, 0), 0))
    halo_next = pl.BlockSpec((1, CONV_HALO, BRANCH), lambda i, t: (i, jnp.minimum((t + 1) * per, last), 0))
    return pl.pallas_call(
        _merge_kernel, grid=(b, n // tm),
        in_specs=[tok(BRANCH), tok(BRANCH), halo_prev, halo_next, _layer_spec(conv_w, layer),
                  tok(BRANCH), tok(BRANCH), tok(3 * d), tok(d), _mod_spec(layer, mod_row, d),
                  _layer_spec(g, layer), _layer_spec(wb, layer), _layer_spec(wo, layer), _layer_spec(wr_t, layer)],
        out_specs=[tok(d), tok(d), pl.BlockSpec((1, N_EXPERTS, tm), lambda i, t: (i, 0, t))],
        out_shape=[jax.ShapeDtypeStruct((b, n, d), F32), jax.ShapeDtypeStruct((b, n, d), BF),
                   jax.ShapeDtypeStruct((b, N_EXPERTS, n), F32)],
        compiler_params=_cparams(2, 48),
        name="merge_norm_router",
    )(ab, u, u, u, conv_w, ob, oc, gates, x, mod, g, wb, wo, wr_t)


def _route_consts(nch, seg_rows, cap, gpb):
    r = N_EXPERTS * nch
    idx = np.arange(r)
    seg = idx // seg_rows
    same = seg[:, None] == seg[None, :]
    before = same & (idx[None, :] < idx[:, None])
    lane = np.arange(gpb * LANES)
    same_grp = (lane[:, None] // LANES) == (lane[None, :] // LANES)
    upper = same_grp & (lane[:, None] < lane[None, :])
    base = ((idx % nch) // seg_rows * cap).astype(np.float32).reshape(r, 1)
    return (jnp.asarray(same, BF), jnp.asarray(before, BF), jnp.asarray(upper, BF), jnp.asarray(same_grp, BF),
            jnp.asarray(base))


def _route_kernel(aff_ref, same_ref, before_ref, upper_ref, ones_ref, base_ref, pos_ref, *, cap, gpb):
    bits = jnp.concatenate([lax.bitcast_convert_type(aff_ref[g], jnp.int32) for g in range(gpb)], axis=1)
    same, before, upper, ones = same_ref[...], before_ref[...], upper_ref[...], ones_ref[...]

    def as_mask(cond):
        return jnp.where(cond, 1.0, 0.0).astype(BF)

    def lane_total(col_counts):
        return _mm(col_counts.astype(BF), ones)

    def seg_total(mask_bf):
        return lane_total(_mm(same, mask_bf))

    def prefix(mask_bf):
        return _mm(mask_bf, upper) + lane_total(_mm(before, mask_bf))

    def body(i, thr):
        cand = thr | lax.shift_left(jnp.int32(1), 30 - i)
        return jnp.where(seg_total(as_mask(bits >= cand)) >= cap, cand, thr)

    thr = lax.fori_loop(0, 31, body, jnp.zeros(bits.shape, jnp.int32))
    gt = bits > thr
    eq = bits == thr
    need = cap - seg_total(as_mask(gt))
    sel = jnp.logical_or(gt, jnp.logical_and(eq, prefix(as_mask(eq)) < need))
    slot = jnp.where(sel, prefix(as_mask(sel)) + base_ref[...], -1.0)
    for g in range(gpb):
        pos_ref[g] = slot[:, g * LANES:(g + 1) * LANES]


def _route_call(aff_rows, nch, seg_rows, cap):
    g, r, _ = aff_rows.shape
    gpb = max(n for n in (1, 2, 4) if g % n == 0)
    consts = _route_consts(nch, seg_rows, cap, gpb)
    full = lambda a: pl.BlockSpec(a.shape, lambda i: (0,) * a.ndim)
    return pl.pallas_call(
        functools.partial(_route_kernel, cap=cap, gpb=gpb), grid=(g // gpb,),
        in_specs=[pl.BlockSpec((gpb, r, LANES), lambda i: (i, 0, 0))] + [full(a) for a in consts],
        out_specs=pl.BlockSpec((gpb, r, LANES), lambda i: (i, 0, 0)),
        out_shape=jax.ShapeDtypeStruct((g, r, LANES), F32),
        compiler_params=_cparams(1, 32),
        name="expert_choice_select",
    )(aff_rows, *consts)


def _moe_kernel(h_ref, pos_ref, post_ref, aff_ref, wg_ref, wu_ref, wd_ref, *rest, nslots, rows_per_store, final):
    if final:
        x_ref, mod_ref, gfin_ref, y_ref, x_all = rest
    else:
        (y_ref,) = rest
    e = pl.program_id(1)
    if final:
        xrows = x_ref.shape[1]
        x_all[pl.ds(pl.multiple_of(e * xrows, xrows), xrows), :] = x_ref[0]

    @pl.when(e == 0)
    def _():
        y_ref[...] = jnp.zeros_like(y_ref)

    pos = pos_ref[0, 0]
    aff = aff_ref[0, 0]
    nch = pos.shape[0]
    slot_col = lax.broadcasted_iota(jnp.int32, (nslots, LANES), 0).astype(F32)
    onehot, gate = [], jnp.zeros((nslots, 1), F32)
    for c in range(nch):
        hit = slot_col == pos[c:c + 1, :]
        onehot.append(jnp.where(hit, 1.0, 0.0).astype(BF))
        gate = gate + jnp.sum(jnp.where(hit, aff[c:c + 1, :], 0.0), axis=1, keepdims=True)
    xe = _mm(jnp.concatenate(onehot, axis=1), h_ref[0]).astype(BF)
    a = _mm(xe, wg_ref[0])
    u = _mm(xe, wu_ref[0])
    mid = ((a * jax.nn.sigmoid(a)) * u).astype(BF)
    ye = (_mm(mid, wd_ref[0]) * gate).astype(BF)

    post = post_ref[0, 0]
    slot_row = lax.broadcasted_iota(jnp.int32, (LANES, nslots), 1).astype(F32)
    per = rows_per_store // LANES
    for c0 in range(0, nch, per):
        scat = jnp.concatenate(
            [jnp.where(post[:, c:c + 1] == slot_row, 1.0, 0.0).astype(BF) for c in range(c0, min(c0 + per, nch))],
            axis=0)
        lo = c0 * LANES
        y_ref[0, lo:lo + scat.shape[0], :] += _mm(scat, ye)

    if final:
        @pl.when(e == N_EXPERTS - 1)
        def _():
            gt2 = mod_ref[0][5:6, :]
            for lo in range(0, y_ref.shape[1], rows_per_store):
                rows = slice(lo, lo + rows_per_store)
                x = x_all[rows, :] + gt2 * y_ref[0, rows, :]
                ms = jnp.mean(x * x, axis=-1, keepdims=True)
                y_ref[0, rows, :] = (x * lax.rsqrt(ms + RMS_EPS)) * gfin_ref[...]


def _moe_call(h2, pos, post, aff, wg, wu, wd, layer, nslots, final=None):
    g, n, d = h2.shape
    nch = n // LANES
    ff = wg.shape[3]
    tile = lambda: pl.BlockSpec((1, 1, nch, LANES), lambda i, e: (i, e, 0, 0))
    args = [h2, pos, post, aff, wg, wu, wd]
    in_specs = [pl.BlockSpec((1, n, d), lambda i, e: (i, 0, 0)),
                tile(), pl.BlockSpec((1, 1, LANES, nch), lambda i, e: (i, e, 0, 0)), tile(),
                pl.BlockSpec((None, 1, d, ff), lambda i, e: (layer, e, 0, 0)),
                pl.BlockSpec((None, 1, d, ff), lambda i, e: (layer, e, 0, 0)),
                pl.BlockSpec((None, 1, ff, d), lambda i, e: (layer, e, 0, 0))]
    scratch = []
    if final is not None:
        x, mod, g_final = final
        assert n % N_EXPERTS == 0
        args += [x, mod, g_final.reshape(1, d)]
        in_specs += [pl.BlockSpec((1, n // N_EXPERTS, d), lambda i, e: (i, e, 0)),
                     pl.BlockSpec((None, 1, 6, d), lambda i, e: (layer, i, 0, 0)),
                     pl.BlockSpec((1, d), lambda i, e: (0, 0))]
        scratch = [pltpu.VMEM((n, d), F32)]
    return pl.pallas_call(
        functools.partial(_moe_kernel, nslots=nslots, rows_per_store=min(n, 512), final=final is not None),
        grid=(g, N_EXPERTS), in_specs=in_specs,
        out_specs=pl.BlockSpec((1, n, d), lambda i, e: (i, 0, 0)),
        out_shape=jax.ShapeDtypeStruct((g, n, d), F32),
        scratch_shapes=scratch,
        compiler_params=_cparams(2, 58),
        name="expert_ffn",
    )(*args)


def _expert_choice(h2, aff, wg, wu, wd, layer, group, final=None):
    b, n, d = h2.shape
    assert final is None or group == 1
    cap = CAPACITY_FACTOR * n // N_EXPERTS
    g = b // group
    ng = group * n
    nch = ng // LANES
    aff_g = aff.reshape(g, group, N_EXPERTS, n).transpose(0, 2, 1, 3).reshape(g, N_EXPERTS * nch, LANES)
    pos = _route_call(aff_g, nch, n // LANES, cap)
    pos4 = pos.reshape(g, N_EXPERTS, nch, LANES)
    y = _moe_call(h2.reshape(g, ng, d), pos4, pos4.transpose(0, 1, 3, 2),
                  aff_g.reshape(g, N_EXPERTS, nch, LANES), wg, wu, wd, layer, group * cap, final)
    return y.reshape(b, n, d)


def _rope_tables(s):
    t = np.arange(s)
    row = (t // GRID_W).astype(np.float32)
    col = (t % GRID_W).astype(np.float32)
    nfreq = HEAD_DIM // 4
    inv = (ROPE_BASE ** (-jnp.arange(nfreq, dtype=F32) / nfreq))
    ang = jnp.concatenate([jnp.asarray(row)[:, None] * inv, jnp.asarray(col)[:, None] * inv], axis=-1)
    cos, sin = jnp.cos(ang), jnp.sin(ang)
    cosw = jnp.concatenate([cos, cos] * 2, axis=-1)
    sinw = jnp.concatenate([-sin, sin] * 2, axis=-1)
    return cosw, sinw


def _ctx_group(b, l):
    g = max(1, min(b, 2048 // l))
    while b % g:
        g -= 1
    return g


def kernel(x, c, ctx, c_ctx, w_ada, b_ada, g_mix, w_in, conv_w, na_rpb, sw_sink, w_branch, w_out,
           g_ffn, w_router, w_exp_gate, w_exp_up, w_exp_down, g_final):
    bsz, s, d = x.shape
    l = ctx.shape[1]
    depth = w_ada.shape[0]
    assert d == 1024 and s % (NA_QROWS * GRID_W) == 0 and s // GRID_W >= NA_WIN_ROWS and l % LANES == 0
    tm = 512 if s % 512 == 0 else 256
    tmc = min(256, l)

    mod_rows = -(-(bsz + 1) // 8) * 8
    cs = jnp.zeros((mod_rows, d), F32).at[:bsz].set(c).at[bsz].set(c_ctx)
    mod = _ada_call(cs, w_ada.astype(BF), b_ada).reshape(depth, mod_rows, 6, d)

    assert w_in.shape[-1] == C_END
    w_ext = w_in.astype(BF)
    wb, wo = w_branch.astype(BF), w_out.astype(BF)
    wr_t = jnp.swapaxes(w_router, 1, 2).astype(BF)
    wg, wu, wd = w_exp_gate.astype(BF), w_exp_up.astype(BF), w_exp_down.astype(BF)
    g_mix3, g_ffn3 = g_mix.reshape(depth, 1, d), g_ffn.reshape(depth, 1, d)
    sink_flat = sw_sink.reshape(-1)
    cosw, sinw = _rope_tables(s)
    first, cls_of, cls_tiles = _na_layout(s // GRID_W)
    bias = _na_bias(na_rpb, cls_tiles)
    cgroup = _ctx_group(bsz, l)

    xc = ctx
    y = yc = None
    for li in range(depth):
        last = li == depth - 1
        x, (ab, u, naq, nak, nav, swq, swk, swv, gates, nak8, nav8) = _inproj_call(
            x, y, mod, li, None, g_mix3, w_ext, cosw, sinw, tm=tm)
        xc, (cab, cu, cnaq, cnak, cnav, cswq, cswk, cswv, cgates) = _inproj_call(
            xc, yc, mod, li, bsz, g_mix3, w_ext, None, None, tm=tmc)

        o_b = _na_call(naq, nak, nav, nak8, nav8, cnak, cnav, bias, li, first, cls_of)
        o_c = _sw_call(swq, swk, swv, cswk, cswv, sink_flat, li)
        x, h2, aff = _merge_call(ab, u, conv_w, o_b, o_c, gates, x, mod, li, None, g_ffn3, wb, wo, wr_t, tm=tm)
        y = _expert_choice(h2, aff, wg, wu, wd, li, 1, final=(x, mod, g_final) if last else None)
        if not last:
            co_b, co_c = _ctx_attn_call(cnaq, cnak, cnav, cswq, cswk, cswv, sink_flat, li)
            xc, h2c, affc = _merge_call(cab, cu, conv_w, co_b, co_c, cgates, xc, mod, li, bsz, g_ffn3, wb, wo, wr_t,
                                        tm=tmc)
            yc = _expert_choice(h2c, affc, wg, wu, wd, li, cgroup)
    return y
```
